```python
import math
import jax, jax.numpy as jnp
from jax import lax
import numpy as np


D_MODEL = 1024
BATCH = 32
SEQ = 2048
DEPTH = 4

CHUNK = 64
N_META = 16
NORM_EPS = 1e-6

N_A_LAYERS = DEPTH // 2
N_B_LAYERS = DEPTH - N_A_LAYERS
N_DENSE = (DEPTH + 1) // 2
N_MOE = DEPTH // 2

D_RNN = D_MODEL
N_RNN_BLOCKS = 8
RNN_BLOCK = D_RNN // N_RNN_BLOCKS
CONV_WIDTH = 4
LRU_C = 8.0

N_HEADS = 16
QK_NOPE = 64
QK_ROPE = 32
V_HEAD = 64
Q_LORA = 384
KV_LORA = 256
ROPE_THETA = 10000.0
Q_BLOCK = 128
ATTN_SCALE = 1.0 / math.sqrt(QK_NOPE + QK_ROPE)
MASK_VALUE = -1e30

D_FF = 2816
N_EXPERTS = 8
TOP_K = 2
D_FF_EXPERT = 3584

kernel_name = 'hybrid_rglru_mla_moe_yoco_trunk'


def rmsnorm(x, g):
    xf = x.astype(jnp.float32)
    y = xf * lax.rsqrt(jnp.mean(xf * xf, axis=-1, keepdims=True) + NORM_EPS)
    return (y * g.astype(jnp.float32)).astype(x.dtype)


def chunk_ids(pos):
    return jnp.where(pos < N_META, 0, 1 + (pos - N_META) // CHUNK)


def rope(x, cos, sin):
    xf = x.astype(jnp.float32)
    half = xf.shape[-1] // 2
    x1, x2 = xf[..., :half], xf[..., half:]
    out = jnp.concatenate([x1 * cos - x2 * sin, x2 * cos + x1 * sin], axis=-1)
    return out.astype(x.dtype)


def causal_depthwise_conv(x, w, b):
    c = x.shape[-1]
    xp = jnp.pad(x, ((0, 0), (CONV_WIDTH - 1, 0), (0, 0)))
    y = lax.conv_general_dilated(xp, w[:, None, :].astype(x.dtype), window_strides=(1,), padding='VALID',
                                 dimension_numbers=('NWC', 'WIO', 'NWC'), feature_group_count=c)
    return y + b


def rg_lru(x, w_a, b_a, w_i, b_i, lam):
    bsz, t, c = x.shape
    xb = x.reshape(bsz, t, N_RNN_BLOCKS, RNN_BLOCK)
    gate_r = jax.nn.sigmoid(jnp.einsum('btnc,ncd->btnd', xb, w_a).reshape(bsz, t, c) + b_a)
    gate_i = jax.nn.sigmoid(jnp.einsum('btnc,ncd->btnd', xb, w_i).reshape(bsz, t, c) + b_i)
    log_a = -LRU_C * gate_r.astype(jnp.float32) * jax.nn.softplus(-lam.astype(jnp.float32))
    a = jnp.exp(log_a)
    mult = jnp.sqrt(-jnp.expm1(2.0 * log_a))
    u = mult * (gate_i * x).astype(jnp.float32)

    def combine(left, right):
        a1, b1 = left
        a2, b2 = right
        return a1 * a2, a2 * b1 + b2

    _, h = lax.associative_scan(combine, (a, u), axis=1)
    return h.astype(x.dtype)


def recurrent_mixer(h, w_in, conv_w, conv_b, w_a, b_a, w_i, b_i, lam, w_out):
    proj = h @ w_in
    y_gate = jax.nn.gelu(proj[..., :D_RNN], approximate=True)
    x_rnn = causal_depthwise_conv(proj[..., D_RNN:], conv_w, conv_b)
    x_rnn = rg_lru(x_rnn, w_a, b_a, w_i, b_i, lam)
    return (x_rnn * y_gate) @ w_out


def shared_latent_kv(x, g_src, w_down, g_latent, w_up, cos, sin):
    bsz, t, _ = x.shape
    hs = rmsnorm(x, g_src)
    ckv = hs @ w_down
    c_lat = rmsnorm(ckv[..., :KV_LORA], g_latent)
    k_pe = rope(ckv[..., KV_LORA:], cos, sin)
    kv = (c_lat @ w_up).reshape(bsz, t, N_HEADS, QK_NOPE + V_HEAD)
    return kv[..., :QK_NOPE], k_pe, kv[..., QK_NOPE:]


def mla_mixer(h, k_nope, k_pe, v, w_dq, g_q, w_uq, w_o, cos, sin, cid):
    bsz, t, _ = h.shape
    c_q = rmsnorm(h @ w_dq, g_q)
    q = (c_q @ w_uq).reshape(bsz, t, N_HEADS, QK_NOPE + QK_ROPE)
    q_nope = q[..., :QK_NOPE]
    q_pe = rope(q[..., QK_NOPE:], cos[:, None, :], sin[:, None, :])
    n_blk = -(-t // Q_BLOCK)
    tp = n_blk * Q_BLOCK
    pad = ((0, 0), (0, tp - t), (0, 0), (0, 0))
    qn_b = jnp.pad(q_nope, pad).reshape(bsz, n_blk, Q_BLOCK, N_HEADS, QK_NOPE).transpose(1, 0, 2, 3, 4)
    qp_b = jnp.pad(q_pe, pad).reshape(bsz, n_blk, Q_BLOCK, N_HEADS, QK_ROPE).transpose(1, 0, 2, 3, 4)
    cid_q = chunk_ids(jnp.arange(tp)).reshape(n_blk, Q_BLOCK)

    def attend(args):
        qn, qp, cq = args
        s = (jnp.einsum('bqhd,bkhd->bhqk', qn, k_nope, preferred_element_type=jnp.float32)
             + jnp.einsum('bqhr,bkr->bhqk', qp, k_pe, preferred_element_type=jnp.float32)) * ATTN_SCALE
        mask = cid[None, :] <= cq[:, None]
        p = jax.nn.softmax(jnp.where(mask, s, MASK_VALUE), axis=-1).astype(v.dtype)
        return jnp.einsum('bhqk,bkhd->bqhd', p, v)

    o = lax.map(attend, (qn_b, qp_b, cid_q))
    o = o.transpose(1, 0, 2, 3, 4).reshape(bsz, tp, N_HEADS * V_HEAD)[:, :t]
    return o @ w_o


def swiglu(h, w1, w3, w2):
    return (jax.nn.silu(h @ w1) * (h @ w3)) @ w2


def moe_swiglu(h, w_router, w1, w3, w2):
    logits = (h @ w_router).astype(jnp.float32)
    top_v, top_i = lax.top_k(logits, TOP_K)
    gates = jax.nn.softmax(top_v, axis=-1)
    dense_gate = jnp.sum(jax.nn.one_hot(top_i, N_EXPERTS, dtype=jnp.float32) * gates[..., None], axis=-2)
    out = jnp.zeros(h.shape, jnp.float32)
    for e in range(N_EXPERTS):
        y = swiglu(h, w1[e], w3[e], w2[e]).astype(jnp.float32)
        out = out + dense_gate[..., e:e + 1] * y
    return out.astype(h.dtype)


def setup_inputs(seed: int = 0) -> dict:
    key = jax.random.key(seed)
    ks = jax.random.split(key, 32)
    f32 = jnp.float32

    def nrm(k, shape, fan_in):
        return jax.random.normal(k, shape, f32) * (fan_in ** -0.5)

    def gain(k, shape):
        return 1.0 + 0.02 * jax.random.normal(k, shape, f32)

    def small(k, shape):
        return 0.01 * jax.random.normal(k, shape, f32)

    u = jax.random.uniform(ks[12], (N_A_LAYERS, D_RNN), f32, minval=0.9, maxval=0.999)
    s = u ** (1.0 / LRU_C)
    return {
        'x': jax.random.normal(ks[0], (BATCH, SEQ, D_MODEL), f32),
        'meta_tokens': jax.random.normal(ks[1], (N_META, D_MODEL), f32),
        'norm_mix': gain(ks[2], (DEPTH, D_MODEL)),
        'norm_ffn': gain(ks[3], (DEPTH, D_MODEL)),
        'norm_final': gain(ks[4], (D_MODEL,)),
        'rnn_w_in': nrm(ks[5], (N_A_LAYERS, D_MODEL, 2 * D_RNN), D_MODEL),
        'rnn_conv_w': nrm(ks[6], (N_A_LAYERS, CONV_WIDTH, D_RNN), CONV_WIDTH),
        'rnn_conv_b': small(ks[7], (N_A_LAYERS, D_RNN)),
        'rnn_w_a': nrm(ks[8], (N_A_LAYERS, N_RNN_BLOCKS, RNN_BLOCK, RNN_BLOCK), RNN_BLOCK),
        'rnn_b_a': small(ks[9], (N_A_LAYERS, D_RNN)),
        'rnn_w_i': nrm(ks[10], (N_A_LAYERS, N_RNN_BLOCKS, RNN_BLOCK, RNN_BLOCK), RNN_BLOCK),
        'rnn_b_i': small(ks[11], (N_A_LAYERS, D_RNN)),
        'rnn_lambda': jnp.log(s) - jnp.log1p(-s),
        'rnn_w_out': nrm(ks[13], (N_A_LAYERS, D_RNN, D_MODEL), D_RNN),
        'kv_norm_src': gain(ks[14], (D_MODEL,)),
        'kv_w_down': nrm(ks[15], (D_MODEL, KV_LORA + QK_ROPE), D_MODEL),
        'kv_latent_norm': gain(ks[16], (KV_LORA,)),
        'kv_w_up': nrm(ks[17], (KV_LORA, N_HEADS * (QK_NOPE + V_HEAD)), KV_LORA),
        'q_w_down': nrm(ks[18], (N_B_LAYERS, D_MODEL, Q_LORA), D_MODEL),
        'q_latent_norm': gain(ks[19], (N_B_LAYERS, Q_LORA)),
        'q_w_up': nrm(ks[20], (N_B_LAYERS, Q_LORA, N_HEADS * (QK_NOPE + QK_ROPE)), Q_LORA),
        'attn_w_out': nrm(ks[21], (N_B_LAYERS, N_HEADS * V_HEAD, D_MODEL), N_HEADS * V_HEAD),
        'ffn_w1': nrm(ks[22], (N_DENSE, D_MODEL, D_FF), D_MODEL),
        'ffn_w3': nrm(ks[23], (N_DENSE, D_MODEL, D_FF), D_MODEL),
        'ffn_w2': nrm(ks[24], (N_DENSE, D_FF, D_MODEL), D_FF),
        'moe_router': nrm(ks[25], (N_MOE, D_MODEL, N_EXPERTS), D_MODEL),
        'moe_w1': nrm(ks[26], (N_MOE, N_EXPERTS, D_MODEL, D_FF_EXPERT), D_MODEL),
        'moe_w3': nrm(ks[27], (N_MOE, N_EXPERTS, D_MODEL, D_FF_EXPERT), D_MODEL),
        'moe_w2': nrm(ks[28], (N_MOE, N_EXPERTS, D_FF_EXPERT, D_MODEL), D_FF_EXPERT),
    }


def reference(x, meta_tokens, norm_mix, norm_ffn, norm_final, rnn_w_in, rnn_conv_w, rnn_conv_b,
              rnn_w_a, rnn_b_a, rnn_w_i, rnn_b_i, rnn_lambda, rnn_w_out, kv_norm_src, kv_w_down,
              kv_latent_norm, kv_w_up, q_w_down, q_latent_norm, q_w_up, attn_w_out,
              ffn_w1, ffn_w3, ffn_w2, moe_router, moe_w1, moe_w3, moe_w2):
    bsz = x.shape[0]
    meta = jnp.broadcast_to(meta_tokens.astype(x.dtype)[None], (bsz, N_META, x.shape[-1]))
    h = jnp.concatenate([meta, x], axis=1)
    t = h.shape[1]
    pos = jnp.arange(t)
    cid = chunk_ids(pos)
    inv_freq = ROPE_THETA ** (-jnp.arange(0, QK_ROPE, 2, dtype=jnp.float32) / QK_ROPE)
    ang = pos.astype(jnp.float32)[:, None] * inv_freq[None, :]
    cos, sin = jnp.cos(ang), jnp.sin(ang)

    k_nope = k_pe = v = None
    for layer in range(DEPTH):
        hn = rmsnorm(h, norm_mix[layer])
        if layer < N_A_LAYERS:
            a = layer
            h = h + recurrent_mixer(hn, rnn_w_in[a], rnn_conv_w[a], rnn_conv_b[a], rnn_w_a[a], rnn_b_a[a],
                                    rnn_w_i[a], rnn_b_i[a], rnn_lambda[a], rnn_w_out[a])
        else:
            b = layer - N_A_LAYERS
            if b == 0:
                k_nope, k_pe, v = shared_latent_kv(h, kv_norm_src, kv_w_down, kv_latent_norm, kv_w_up, cos, sin)
                hn = rmsnorm(h, norm_mix[layer])
            h = h + mla_mixer(hn, k_nope, k_pe, v, q_w_down[b], q_latent_norm[b], q_w_up[b],
                              attn_w_out[b], cos, sin, cid)
        hf = rmsnorm(h, norm_ffn[layer])
        if layer % 2 == 0:
            e = layer // 2
            h = h + swiglu(hf, ffn_w1[e], ffn_w3[e], ffn_w2[e])
        else:
            m = layer // 2
            h = h + moe_swiglu(hf, moe_router[m], moe_w1[m], moe_w3[m], moe_w2[m])
    h = rmsnorm(h, norm_final)
    return h[:, N_META:]
```

```python
import functools
import math

import jax
import jax.numpy as jnp
from jax import lax
from jax.experimental import pallas as pl
from jax.experimental.pallas import tpu as pltpu

D_MODEL = 1024
N_META = 16
CHUNK = 64
NORM_EPS = 1e-6
DEPTH = 4
N_A_LAYERS = DEPTH // 2

D_RNN = D_MODEL
N_RNN_BLOCKS = 8
RNN_BLOCK = D_RNN // N_RNN_BLOCKS
CONV_WIDTH = 4
LRU_C = 8.0

N_HEADS = 16
QK_NOPE = 64
QK_ROPE = 32
V_HEAD = 64
Q_LORA = 384
KV_LORA = 256
ROPE_THETA = 10000.0
ATTN_SCALE = 1.0 / math.sqrt(QK_NOPE + QK_ROPE)
MASK_VALUE = -1e30

N_EXPERTS = 8

LANES = 128
SUBLANES = 8
VMEM_LIMIT = 56 * 1024 * 1024

HEAD_PAD = 128
Q_BLOCK = 256
TT = 688
F32 = jnp.float32
BF16 = jnp.bfloat16


def _cparams(n_axes):
    return pltpu.CompilerParams(dimension_semantics=("arbitrary",) * n_axes,
                                vmem_limit_bytes=VMEM_LIMIT)


def _rms(x, g):
    return x * lax.rsqrt(jnp.mean(x * x, axis=-1, keepdims=True) + NORM_EPS) * g


def _dot(a, b):
    return jnp.dot(a, b, preferred_element_type=F32)


def _dot_nt(a, b):
    return lax.dot_general(a, b, (((1,), (1,)), ((), ())), preferred_element_type=F32)


def _rnn_kernel(h_ref, g_ref, win_ref, cw_ref, cb_ref, wg_ref, ba_ref, bi_ref, lam_ref, wout_ref,
                o_ref, xpad_ref, yg_ref, a_ref, u_ref, carry_ref):
    t = pl.program_id(1)
    tt = h_ref.shape[1]

    @pl.when(t == 0)
    def _():
        xpad_ref[0:SUBLANES, :] = jnp.zeros((SUBLANES, D_RNN), F32)
        carry_ref[...] = jnp.zeros_like(carry_ref)

    x = h_ref[0]
    hn = _rms(x, g_ref[...]).astype(BF16)
    yg_ref[...] = jax.nn.gelu(_dot(hn, win_ref[:, :D_RNN]), approximate=True)
    xpad_ref[SUBLANES:, :] = _dot(hn, win_ref[:, D_RNN:])

    xc = cb_ref[...] + cw_ref[CONV_WIDTH - 1:CONV_WIDTH, :] * xpad_ref[SUBLANES:, :]
    for j in range(CONV_WIDTH - 1):
        off = SUBLANES - (CONV_WIDTH - 1) + j
        xc = xc + cw_ref[j:j + 1, :] * xpad_ref[off:off + tt, :]
    xpad_ref[0:SUBLANES, :] = xpad_ref[tt:tt + SUBLANES, :]

    xcb = xc.astype(BF16)
    log_coef = -LRU_C * jax.nn.softplus(-lam_ref[...])
    pair = 2 * RNN_BLOCK
    for p in range(N_RNN_BLOCKS // 2):
        sl = slice(p * pair, (p + 1) * pair)
        gates = _dot(xcb[:, sl], wg_ref[p])
        gate_r = jax.nn.sigmoid(gates[:, :pair] + ba_ref[:, sl])
        gate_i = jax.nn.sigmoid(gates[:, pair:] + bi_ref[:, sl])
        log_a = log_coef[:, sl] * gate_r
        a_ref[:, sl] = jnp.exp(log_a)
        th = jnp.tanh(log_a)
        u_ref[:, sl] = jnp.sqrt(-2.0 * th / (1.0 - th)) * (gate_i * xc[:, sl])

    row = lax.broadcasted_iota(jnp.int32, (SUBLANES, D_RNN), 0)

    def group(i, carry):
        r0 = pl.multiple_of(i * SUBLANES, SUBLANES)
        a = a_ref[pl.ds(r0, SUBLANES), :]
        u = u_ref[pl.ds(r0, SUBLANES), :]
        for s in (1, 2, 4):
            keep = row >= s
            u = jnp.where(keep, a * pltpu.roll(u, s, axis=0) + u, u)
            a = jnp.where(keep, a * pltpu.roll(a, s, axis=0), a)
        hs = a * carry + u
        u_ref[pl.ds(r0, SUBLANES), :] = hs
        return hs[SUBLANES - 1:SUBLANES, :]

    carry_ref[...] = lax.fori_loop(0, tt // SUBLANES, group, carry_ref[...])

    y = (u_ref[...] * yg_ref[...]).astype(BF16)
    o_ref[0] = x + _dot(y, wout_ref[...])


def _rnn_layer(h, g, w_in, conv_w, conv_b, w_gate, b_a, b_i, lam, w_out):
    bsz, t, d = h.shape
    nt = t // TT
    full = lambda shape: pl.BlockSpec(shape, lambda b, i: (0,) * len(shape))
    tile = pl.BlockSpec((1, TT, d), lambda b, i: (b, i, 0))
    return pl.pallas_call(
        _rnn_kernel,
        grid=(bsz, nt),
        in_specs=[tile, full((1, d)), full((d, 2 * D_RNN)), full((CONV_WIDTH, D_RNN)), full((1, D_RNN)),
                  full(w_gate.shape), full((1, D_RNN)), full((1, D_RNN)), full((1, D_RNN)),
                  full((D_RNN, d))],
        out_specs=tile,
        out_shape=jax.ShapeDtypeStruct(h.shape, F32),
        scratch_shapes=[pltpu.VMEM((TT + SUBLANES, D_RNN), F32), pltpu.VMEM((TT, D_RNN), F32),
                        pltpu.VMEM((TT, D_RNN), F32), pltpu.VMEM((TT, D_RNN), F32),
                        pltpu.VMEM((1, D_RNN), F32)],
        compiler_params=_cparams(2),
        name="rnn_mixer",
    )(h, g, w_in, conv_w, conv_b, w_gate, b_a, b_i, lam, w_out)


def _ffn_kernel(h_ref, g_ref, wr_ref, w1_ref, w3_ref, w2_ref, o_ref, hf_ref, gate_ref, acc_ref,
                *, n_experts):
    e = pl.program_id(2)
    f = pl.program_id(3)
    first = jnp.logical_and(e == 0, f == 0)
    last = jnp.logical_and(e == pl.num_programs(2) - 1, f == pl.num_programs(3) - 1)

    @pl.when(first)
    def _():
        hf = _rms(h_ref[0], g_ref[...])
        hf_ref[...] = hf.astype(BF16)
        acc_ref[...] = jnp.zeros_like(acc_ref)
        if n_experts > 1:
            logits = jnp.dot(hf, wr_ref[...], preferred_element_type=F32,
                             precision=lax.Precision.HIGHEST)
            idx = lax.broadcasted_iota(jnp.int32, logits.shape, 1)
            m1 = jnp.max(logits, axis=1, keepdims=True)
            i1 = jnp.min(jnp.where(logits == m1, idx, n_experts), axis=1, keepdims=True)
            rest = jnp.where(idx == i1, -jnp.inf, logits)
            m2 = jnp.max(rest, axis=1, keepdims=True)
            i2 = jnp.min(jnp.where(rest == m2, idx, n_experts), axis=1, keepdims=True)
            ex = jnp.exp(m2 - m1)
            inv = 1.0 / (1.0 + ex)
            gate_ref[...] = jnp.where(idx == i1, inv, 0.0) + jnp.where(idx == i2, ex * inv, 0.0)

    hf = hf_ref[...]
    act = jax.nn.silu(_dot(hf, w1_ref[0])) * _dot(hf, w3_ref[0])
    y = _dot(act.astype(BF16), w2_ref[0])
    if n_experts > 1:
        idx = lax.broadcasted_iota(jnp.int32, gate_ref.shape, 1)
        y = y * jnp.sum(jnp.where(idx == e, gate_ref[...], 0.0), axis=1, keepdims=True)
    acc_ref[...] += y

    @pl.when(last)
    def _():
        o_ref[0] = h_ref[0] + acc_ref[...]


def _ffn_layer(h, g, w_router, w1, w3, w2, tf):
    bsz, t, d = h.shape
    n_experts, _, d_ff = w1.shape
    nt, nf = t // TT, d_ff // tf
    tile = pl.BlockSpec((1, TT, d), lambda b, i, e, f: (b, i, 0))
    return pl.pallas_call(
        functools.partial(_ffn_kernel, n_experts=n_experts),
        grid=(bsz, nt, n_experts, nf),
        in_specs=[tile,
                  pl.BlockSpec((1, d), lambda b, i, e, f: (0, 0)),
                  pl.BlockSpec(w_router.shape, lambda b, i, e, f: (0, 0)),
                  pl.BlockSpec((1, d, tf), lambda b, i, e, f: (e, 0, f)),
                  pl.BlockSpec((1, d, tf), lambda b, i, e, f: (e, 0, f)),
                  pl.BlockSpec((1, tf, d), lambda b, i, e, f: (e, f, 0))],
        out_specs=tile,
        out_shape=jax.ShapeDtypeStruct(h.shape, F32),
        scratch_shapes=[pltpu.VMEM((TT, d), BF16), pltpu.VMEM((TT, n_experts), F32),
                        pltpu.VMEM((TT, d), F32)],
        compiler_params=_cparams(4),
        name="ffn_moe" if n_experts > 1 else "ffn_dense",
    )(h, g, w_router, w1, w3, w2)


def _kv_kernel(h_ref, g_ref, wd_ref, gl_ref, cs_ref, wk_ref, wv_ref, k_ref, v_ref):
    hs = _rms(h_ref[0], g_ref[...]).astype(BF16)
    ckv = _dot(hs, wd_ref[...])
    c_lat = _rms(ckv[:, :KV_LORA], gl_ref[...])
    pe = ckv[:, KV_LORA:KV_LORA + QK_ROPE]
    pe_rot = ckv[:, KV_LORA + QK_ROPE:]
    k_pe = pe * cs_ref[:, :QK_ROPE] + pe_rot * cs_ref[:, QK_ROPE:]
    lat = jnp.concatenate([c_lat, k_pe], axis=1).astype(BF16)
    k_ref[0] = _dot(lat, wk_ref[...]).astype(BF16)
    v_ref[0] = _dot(lat[:, :KV_LORA], wv_ref[...]).astype(BF16)


def _kv_proj(h, g_src, w_down_aug, g_latent, cs_k, w_k, w_v):
    bsz, t, d = h.shape
    nt = t // TT
    full = lambda shape: pl.BlockSpec(shape, lambda b, i: (0,) * len(shape))
    return pl.pallas_call(
        _kv_kernel,
        grid=(bsz, nt),
        in_specs=[pl.BlockSpec((1, TT, d), lambda b, i: (b, i, 0)), full((1, d)), full(w_down_aug.shape),
                  full((1, KV_LORA)), pl.BlockSpec((TT, 2 * QK_ROPE), lambda b, i: (i, 0)),
                  full(w_k.shape), full(w_v.shape)],
        out_specs=[pl.BlockSpec((1, TT, N_HEADS * HEAD_PAD), lambda b, i: (b, i, 0)),
                   pl.BlockSpec((1, TT, N_HEADS * V_HEAD), lambda b, i: (b, i, 0))],
        out_shape=[jax.ShapeDtypeStruct((bsz, t, N_HEADS * HEAD_PAD), BF16),
                   jax.ShapeDtypeStruct((bsz, t, N_HEADS * V_HEAD), BF16)],
        compiler_params=_cparams(2),
        name="kv_proj",
    )(h, g_src, w_down_aug, g_latent, cs_k, w_k, w_v)


def _q_kernel(h_ref, g_ref, wdq_ref, gq_ref, wuq_ref, ct_ref, st_ref, q_ref):
    hn = _rms(h_ref[0], g_ref[...]).astype(BF16)
    c_q = _rms(_dot(hn, wdq_ref[...]), gq_ref[...]).astype(BF16)
    q = _dot(c_q, wuq_ref[...])
    width = q.shape[1]
    ct = jnp.tile(ct_ref[...], (1, N_HEADS))
    st = jnp.tile(st_ref[...], (1, N_HEADS))
    q_ref[0] = (q * ct + pltpu.roll(q, width - QK_ROPE, axis=1) * st).astype(BF16)


def _q_proj(h, g, w_dq, g_q, w_uq_aug, ctab, stab):
    bsz, t, d = h.shape
    nt = t // TT
    full = lambda shape: pl.BlockSpec(shape, lambda b, i: (0,) * len(shape))
    tab = pl.BlockSpec((TT, HEAD_PAD), lambda b, i: (i, 0))
    return pl.pallas_call(
        _q_kernel,
        grid=(bsz, nt),
        in_specs=[pl.BlockSpec((1, TT, d), lambda b, i: (b, i, 0)), full((1, d)), full(w_dq.shape),
                  full((1, Q_LORA)), full(w_uq_aug.shape), tab, tab],
        out_specs=pl.BlockSpec((1, TT, N_HEADS * HEAD_PAD), lambda b, i: (b, i, 0)),
        out_shape=jax.ShapeDtypeStruct((bsz, t, N_HEADS * HEAD_PAD), BF16),
        compiler_params=_cparams(2),
        name="q_proj",
    )(h, g, w_dq, g_q, w_uq_aug, ctab, stab)


def _attn_kernel(q_ref, k_ref, v_ref, o_ref):
    n_qblk = (q_ref.shape[1] - N_META) // Q_BLOCK
    heads = (0, 1)
    qk_sl = [slice(hh * HEAD_PAD, (hh + 1) * HEAD_PAD) for hh in heads]
    v_sl = [slice(hh * V_HEAD, (hh + 1) * V_HEAD) for hh in heads]
    k_meta = [k_ref[0, 0:N_META, qk_sl[hh]] for hh in heads]
    v_meta = [v_ref[0, 0:N_META, v_sl[hh]] for hh in heads]

    def start(q, hh):
        s = _dot_nt(q, k_meta[hh])
        m = jnp.max(s, axis=1, keepdims=True)
        p = jnp.exp(s - m)
        return m, jnp.sum(p, axis=1, keepdims=True), _dot(p.astype(BF16), v_meta[hh])

    def update(carry, s, v):
        m, l, acc = carry
        m_new = jnp.maximum(m, jnp.max(s, axis=1, keepdims=True))
        alpha = jnp.exp(m - m_new)
        p = jnp.exp(s - m_new)
        return (m_new, alpha * l + jnp.sum(p, axis=1, keepdims=True),
                alpha * acc + _dot(p.astype(BF16), v))

    outs = []
    for hh in heads:
        _, l, acc = start(q_ref[0, 0:N_META, qk_sl[hh]], hh)
        outs.append(acc / l)
    o_ref[0, 0:N_META, :] = jnp.concatenate(outs, axis=1).astype(BF16)

    rc = lax.broadcasted_iota(jnp.int32, (Q_BLOCK, Q_BLOCK), 0) // CHUNK
    cc = lax.broadcasted_iota(jnp.int32, (Q_BLOCK, Q_BLOCK), 1) // CHUNK
    diag_mask = cc <= rc

    def q_block(i, _):
        r0 = pl.multiple_of(N_META + i * Q_BLOCK, N_META)
        outs = []
        for hh in heads:
            q = q_ref[0, pl.ds(r0, Q_BLOCK), qk_sl[hh]]

            def kv_step(j, carry, q=q, hh=hh):
                c0 = pl.multiple_of(N_META + j * Q_BLOCK, N_META)
                s = _dot_nt(q, k_ref[0, pl.ds(c0, Q_BLOCK), qk_sl[hh]])
                return update(carry, s, v_ref[0, pl.ds(c0, Q_BLOCK), v_sl[hh]])

            carry = lax.fori_loop(0, i, kv_step, start(q, hh))
            s = _dot_nt(q, k_ref[0, pl.ds(r0, Q_BLOCK), qk_sl[hh]])
            s = jnp.where(diag_mask, s, MASK_VALUE)
            _, l, acc = update(carry, s, v_ref[0, pl.ds(r0, Q_BLOCK), v_sl[hh]])
            outs.append(acc / l)
        o_ref[0, pl.ds(r0, Q_BLOCK), :] = jnp.concatenate(outs, axis=1).astype(BF16)
        return 0

    lax.fori_loop(0, n_qblk, q_block, 0)


def _attention(q, k, v):
    bsz, t, _ = q.shape
    qk_spec = pl.BlockSpec((1, t, 2 * HEAD_PAD), lambda b, p: (b, 0, p))
    v_spec = pl.BlockSpec((1, t, 2 * V_HEAD), lambda b, p: (b, 0, p))
    return pl.pallas_call(
        _attn_kernel,
        grid=(bsz, N_HEADS // 2),
        in_specs=[qk_spec, qk_spec, v_spec],
        out_specs=v_spec,
        out_shape=jax.ShapeDtypeStruct((bsz, t, N_HEADS * V_HEAD), BF16),
        compiler_params=_cparams(2),
        name="attention",
    )(q, k, v)


def _oproj_kernel(h_ref, a_ref, w_ref, o_ref):
    o_ref[0] = h_ref[0] + _dot(a_ref[0], w_ref[...])


def _o_proj(h, attn, w_o):
    bsz, t, d = h.shape
    nt = t // TT
    tile = pl.BlockSpec((1, TT, d), lambda b, i: (b, i, 0))
    return pl.pallas_call(
        _oproj_kernel,
        grid=(bsz, nt),
        in_specs=[tile, pl.BlockSpec((1, TT, attn.shape[2]), lambda b, i: (b, i, 0)),
                  pl.BlockSpec(w_o.shape, lambda b, i: (0, 0))],
        out_specs=tile,
        out_shape=jax.ShapeDtypeStruct(h.shape, F32),
        compiler_params=_cparams(2),
        name="o_proj",
    )(h, attn, w_o)


def _final_kernel(h_ref, g_ref, o_ref):
    o_ref[0] = _rms(h_ref[0, N_META:, :], g_ref[...])


def _final_norm(h, g):
    bsz, t, d = h.shape
    return pl.pallas_call(
        _final_kernel,
        grid=(bsz,),
        in_specs=[pl.BlockSpec((1, t, d), lambda b: (b, 0, 0)), pl.BlockSpec((1, d), lambda b: (0, 0))],
        out_specs=pl.BlockSpec((1, t - N_META, d), lambda b: (b, 0, 0)),
        out_shape=jax.ShapeDtypeStruct((bsz, t - N_META, d), F32),
        compiler_params=_cparams(1),
        name="final_norm",
    )(h, g)


def _rot_cols(w):
    half = w.shape[1] // 2
    return jnp.concatenate([-w[:, half:], w[:, :half]], axis=1)


def _gate_pairs(w_a, w_i):
    def bd(w):
        z = jnp.zeros((RNN_BLOCK, RNN_BLOCK), w.dtype)
        return jnp.stack([jnp.block([[w[2 * p], z], [z, w[2 * p + 1]]]) for p in range(N_RNN_BLOCKS // 2)])
    return jnp.concatenate([bd(w_a), bd(w_i)], axis=2).astype(BF16)


def _q_up_aug(w_uq):
    w = w_uq.reshape(Q_LORA, N_HEADS, QK_NOPE + QK_ROPE)
    pe = w[:, :, QK_NOPE:]
    rot = jnp.concatenate([-pe[:, :, QK_ROPE // 2:], pe[:, :, :QK_ROPE // 2]], axis=2)
    return jnp.concatenate([w, rot], axis=2).reshape(Q_LORA, N_HEADS * HEAD_PAD).astype(BF16)


def _kv_up_aug(w_up):
    w = w_up.reshape(KV_LORA, N_HEADS, QK_NOPE + V_HEAD)
    pad = HEAD_PAD - QK_NOPE
    w_k_top = jnp.concatenate([w[:, :, :QK_NOPE], jnp.zeros((KV_LORA, N_HEADS, pad), w.dtype)], axis=2)
    eye = jnp.concatenate([jnp.zeros((QK_ROPE, QK_NOPE), w.dtype), jnp.eye(QK_ROPE, dtype=w.dtype),
                           jnp.zeros((QK_ROPE, pad - QK_ROPE), w.dtype)], axis=1)
    w_k_bot = jnp.broadcast_to(eye[:, None, :], (QK_ROPE, N_HEADS, HEAD_PAD))
    w_k = jnp.concatenate([w_k_top, w_k_bot], axis=0).reshape(KV_LORA + QK_ROPE, N_HEADS * HEAD_PAD)
    w_v = w[:, :, QK_NOPE:].reshape(KV_LORA, N_HEADS * V_HEAD)
    return w_k.astype(BF16), w_v.astype(BF16)


def _rope_tables(t):
    inv_freq = ROPE_THETA ** (-jnp.arange(0, QK_ROPE, 2, dtype=F32) / QK_ROPE)
    ang = jnp.arange(t, dtype=F32)[:, None] * inv_freq[None, :]
    cos = jnp.tile(jnp.cos(ang), (1, 2))
    sin = jnp.tile(jnp.sin(ang), (1, 2))
    cs_k = jnp.concatenate([cos, sin], axis=1)
    ones = jnp.ones((t, QK_NOPE), F32)
    zeros = jnp.zeros((t, HEAD_PAD - QK_NOPE - QK_ROPE), F32)
    ctab = ATTN_SCALE * jnp.concatenate([ones, cos, zeros], axis=1)
    stab = ATTN_SCALE * jnp.concatenate([0.0 * ones, sin, zeros], axis=1)
    return cs_k, ctab, stab


def kernel(x, meta_tokens, norm_mix, norm_ffn, norm_final, rnn_w_in, rnn_conv_w, rnn_conv_b, rnn_w_a, rnn_b_a, rnn_w_i, rnn_b_i, rnn_lambda, rnn_w_out, kv_norm_src, kv_w_down, kv_latent_norm, kv_w_up, q_w_down, q_latent_norm, q_w_up, attn_w_out, ffn_w1, ffn_w3, ffn_w2, moe_router, moe_w1, moe_w3, moe_w2):
    bsz = x.shape[0]
    meta = jnp.broadcast_to(meta_tokens.astype(x.dtype)[None], (bsz, N_META, x.shape[-1]))
    h = jnp.concatenate([meta, x], axis=1)
    t = h.shape[1]
    assert t % TT == 0 and (t - N_META) % Q_BLOCK == 0
    row = lambda v: v.reshape(1, -1)
    cs_k, ctab, stab = _rope_tables(t)
    no_router = jnp.zeros((D_MODEL, 1), F32)

    k = v = None
    for layer in range(DEPTH):
        if layer < N_A_LAYERS:
            a = layer
            h = _rnn_layer(h, row(norm_mix[layer]), rnn_w_in[a].astype(BF16), rnn_conv_w[a], row(rnn_conv_b[a]),
                           _gate_pairs(rnn_w_a[a], rnn_w_i[a]), row(rnn_b_a[a]), row(rnn_b_i[a]),
                           row(rnn_lambda[a]), rnn_w_out[a].astype(BF16))
        else:
            b = layer - N_A_LAYERS
            if b == 0:
                w_down_aug = jnp.concatenate([kv_w_down, _rot_cols(kv_w_down[:, KV_LORA:])], axis=1).astype(BF16)
                w_k, w_v = _kv_up_aug(kv_w_up)
                k, v = _kv_proj(h, row(kv_norm_src), w_down_aug, row(kv_latent_norm), cs_k, w_k, w_v)
            q = _q_proj(h, row(norm_mix[layer]), q_w_down[b].astype(BF16), row(q_latent_norm[b]),
                        _q_up_aug(q_w_up[b]), ctab, stab)
            h = _o_proj(h, _attention(q, k, v), attn_w_out[b].astype(BF16))
        if layer % 2 == 0:
            e = layer // 2
            h = _ffn_layer(h, row(norm_ffn[layer]), no_router, ffn_w1[e:e + 1].astype(BF16),
                           ffn_w3[e:e + 1].astype(BF16), ffn_w2[e:e + 1].astype(BF16), tf=1408)
        else:
            m = layer // 2
            h = _ffn_layer(h, row(norm_ffn[layer]), moe_router[m], moe_w1[m].astype(BF16),
                           moe_w3[m].astype(BF16), moe_w2[m].astype(BF16), tf=1792)
    return _final_norm(h, row(norm_final))
```

```python
import functools
import math

import jax
import jax.numpy as jnp
from jax import lax
from jax.experimental import pallas as pl
from jax.experimental.pallas import tpu as pltpu

D_MODEL = 1024
N_META = 16
CHUNK = 64
NORM_EPS = 1e-6
DEPTH = 4
N_A_LAYERS = DEPTH // 2

D_RNN = D_MODEL
N_RNN_BLOCKS = 8
RNN_BLOCK = D_RNN // N_RNN_BLOCKS
CONV_WIDTH = 4
LRU_C = 8.0

N_HEADS = 16
QK_NOPE = 64
QK_ROPE = 32
V_HEAD = 64
Q_LORA = 384
KV_LORA = 256
ROPE_THETA = 10000.0
ATTN_SCALE = 1.0 / math.sqrt(QK_NOPE + QK_ROPE)
MASK_VALUE = -1e30

N_EXPERTS = 8

LANES = 128
SUBLANES = 8
VMEM_LIMIT = 56 * 1024 * 1024

HEAD_PAD = 128
Q_BLOCK = 512
TT = 688
F32 = jnp.float32
BF16 = jnp.bfloat16


def _cparams(n_axes):
    return pltpu.CompilerParams(dimension_semantics=("arbitrary",) * n_axes,
                                vmem_limit_bytes=VMEM_LIMIT)


def _rms(x, g):
    return x * lax.rsqrt(jnp.mean(x * x, axis=-1, keepdims=True) + NORM_EPS) * g


def _dot(a, b):
    return jnp.dot(a, b, preferred_element_type=F32)


def _dot_nt(a, b):
    return lax.dot_general(a, b, (((1,), (1,)), ((), ())), preferred_element_type=F32)


def _rnn_kernel(h_ref, g_ref, win_ref, cw_ref, cb_ref, wg_ref, ba_ref, bi_ref, lam_ref, wout_ref,
                o_ref, xpad_ref, yg_ref, a_ref, u_ref, carry_ref):
    t = pl.program_id(1)
    tt = h_ref.shape[1]

    @pl.when(t == 0)
    def _():
        xpad_ref[0:SUBLANES, :] = jnp.zeros((SUBLANES, D_RNN), F32)
        carry_ref[...] = jnp.zeros_like(carry_ref)

    x = h_ref[0]
    hn = _rms(x, g_ref[...]).astype(BF16)
    yg_ref[...] = jax.nn.gelu(_dot(hn, win_ref[:, :D_RNN]), approximate=True)
    xpad_ref[SUBLANES:, :] = _dot(hn, win_ref[:, D_RNN:])

    xc = cb_ref[...] + cw_ref[CONV_WIDTH - 1:CONV_WIDTH, :] * xpad_ref[SUBLANES:, :]
    for j in range(CONV_WIDTH - 1):
        off = SUBLANES - (CONV_WIDTH - 1) + j
        xc = xc + cw_ref[j:j + 1, :] * xpad_ref[off:off + tt, :]
    xpad_ref[0:SUBLANES, :] = xpad_ref[tt:tt + SUBLANES, :]

    xcb = xc.astype(BF16)
    log_coef = -LRU_C * jax.nn.softplus(-lam_ref[...])
    pair = 2 * RNN_BLOCK
    for p in range(N_RNN_BLOCKS // 2):
        sl = slice(p * pair, (p + 1) * pair)
        gates = _dot(xcb[:, sl], wg_ref[p])
        gate_r = jax.nn.sigmoid(gates[:, :pair] + ba_ref[:, sl])
        gate_i = jax.nn.sigmoid(gates[:, pair:] + bi_ref[:, sl])
        log_a = log_coef[:, sl] * gate_r
        a_ref[:, sl] = jnp.exp(log_a)
        th = jnp.tanh(log_a)
        u_ref[:, sl] = jnp.sqrt(-2.0 * th / (1.0 - th)) * (gate_i * xc[:, sl])

    row = lax.broadcasted_iota(jnp.int32, (SUBLANES, D_RNN), 0)

    def group(i, carry):
        r0 = pl.multiple_of(i * SUBLANES, SUBLANES)
        a = a_ref[pl.ds(r0, SUBLANES), :]
        u = u_ref[pl.ds(r0, SUBLANES), :]
        for s in (1, 2, 4):
            keep = row >= s
            u = jnp.where(keep, a * pltpu.roll(u, s, axis=0) + u, u)
            a = jnp.where(keep, a * pltpu.roll(a, s, axis=0), a)
        hs = a * carry + u
        u_ref[pl.ds(r0, SUBLANES), :] = hs
        return hs[SUBLANES - 1:SUBLANES, :]

    carry_ref[...] = lax.fori_loop(0, tt // SUBLANES, group, carry_ref[...])

    y = (u_ref[...] * yg_ref[...]).astype(BF16)
    o_ref[0] = x + _dot(y, wout_ref[...])


def _rnn_layer(h, g, w_in, conv_w, conv_b, w_gate, b_a, b_i, lam, w_out):
    bsz, t, d = h.shape
    nt = t // TT
    full = lambda shape: pl.BlockSpec(shape, lambda b, i: (0,) * len(shape))
    tile = pl.BlockSpec((1, TT, d), lambda b, i: (b, i, 0))
    return pl.pallas_call(
        _rnn_kernel,
        grid=(bsz, nt),
        in_specs=[tile, full((1, d)), full((d, 2 * D_RNN)), full((CONV_WIDTH, D_RNN)), full((1, D_RNN)),
                  full(w_gate.shape), full((1, D_RNN)), full((1, D_RNN)), full((1, D_RNN)),
                  full((D_RNN, d))],
        out_specs=tile,
        out_shape=jax.ShapeDtypeStruct(h.shape, F32),
        scratch_shapes=[pltpu.VMEM((TT + SUBLANES, D_RNN), F32), pltpu.VMEM((TT, D_RNN), F32),
                        pltpu.VMEM((TT, D_RNN), F32), pltpu.VMEM((TT, D_RNN), F32),
                        pltpu.VMEM((1, D_RNN), F32)],
        compiler_params=_cparams(2),
        name="rnn_mixer",
    )(h, g, w_in, conv_w, conv_b, w_gate, b_a, b_i, lam, w_out)


def _ffn_kernel(h_ref, g_ref, wr_ref, w1_ref, w3_ref, w2_ref, o_ref, hf_ref, gate_ref, acc_ref,
                *, n_experts):
    e = pl.program_id(2)
    f = pl.program_id(3)
    first = jnp.logical_and(e == 0, f == 0)
    last = jnp.logical_and(e == pl.num_programs(2) - 1, f == pl.num_programs(3) - 1)

    @pl.when(first)
    def _():
        hf = _rms(h_ref[0], g_ref[...])
        hf_ref[...] = hf.astype(BF16)
        acc_ref[...] = jnp.zeros_like(acc_ref)
        if n_experts > 1:
            logits = jnp.dot(hf, wr_ref[...], preferred_element_type=F32,
                             precision=lax.Precision.HIGHEST)
            idx = lax.broadcasted_iota(jnp.int32, logits.shape, 1)
            m1 = jnp.max(logits, axis=1, keepdims=True)
            i1 = jnp.min(jnp.where(logits == m1, idx, n_experts), axis=1, keepdims=True)
            rest = jnp.where(idx == i1, -jnp.inf, logits)
            m2 = jnp.max(rest, axis=1, keepdims=True)
            i2 = jnp.min(jnp.where(rest == m2, idx, n_experts), axis=1, keepdims=True)
            ex = jnp.exp(m2 - m1)
            inv = 1.0 / (1.0 + ex)
            gate_ref[...] = jnp.where(idx == i1, inv, 0.0) + jnp.where(idx == i2, ex * inv, 0.0)

    hf = hf_ref[...]
    act = jax.nn.silu(_dot(hf, w1_ref[0])) * _dot(hf, w3_ref[0])
    y = _dot(act.astype(BF16), w2_ref[0])
    if n_experts > 1:
        idx = lax.broadcasted_iota(jnp.int32, gate_ref.shape, 1)
        y = y * jnp.sum(jnp.where(idx == e, gate_ref[...], 0.0), axis=1, keepdims=True)
    acc_ref[...] += y

    @pl.when(last)
    def _():
        o_ref[0] = h_ref[0] + acc_ref[...]


def _ffn_layer(h, g, w_router, w1, w3, w2, tf):
    bsz, t, d = h.shape
    n_experts, _, d_ff = w1.shape
    nt, nf = t // TT, d_ff // tf
    tile = pl.BlockSpec((1, TT, d), lambda b, i, e, f: (b, i, 0))
    return pl.pallas_call(
        functools.partial(_ffn_kernel, n_experts=n_experts),
        grid=(bsz, nt, n_experts, nf),
        in_specs=[tile,
                  pl.BlockSpec((1, d), lambda b, i, e, f: (0, 0)),
                  pl.BlockSpec(w_router.shape, lambda b, i, e, f: (0, 0)),
                  pl.BlockSpec((1, d, tf), lambda b, i, e, f: (e, 0, f)),
                  pl.BlockSpec((1, d, tf), lambda b, i, e, f: (e, 0, f)),
                  pl.BlockSpec((1, tf, d), lambda b, i, e, f: (e, f, 0))],
        out_specs=tile,
        out_shape=jax.ShapeDtypeStruct(h.shape, F32),
        scratch_shapes=[pltpu.VMEM((TT, d), BF16), pltpu.VMEM((TT, n_experts), F32),
                        pltpu.VMEM((TT, d), F32)],
        compiler_params=_cparams(4),
        name="ffn_moe" if n_experts > 1 else "ffn_dense",
    )(h, g, w_router, w1, w3, w2)


def _kv_kernel(h_ref, g_ref, wd_ref, gl_ref, cs_ref, wk_ref, wv_ref, k_ref, v_ref):
    hs = _rms(h_ref[0], g_ref[...]).astype(BF16)
    ckv = _dot(hs, wd_ref[...])
    c_lat = _rms(ckv[:, :KV_LORA], gl_ref[...])
    pe = ckv[:, KV_LORA:KV_LORA + QK_ROPE]
    pe_rot = ckv[:, KV_LORA + QK_ROPE:]
    k_pe = pe * cs_ref[:, :QK_ROPE] + pe_rot * cs_ref[:, QK_ROPE:]
    lat = jnp.concatenate([c_lat, k_pe], axis=1).astype(BF16)
    k_ref[0] = _dot(lat, wk_ref[...]).astype(BF16)
    v_ref[0] = _dot(lat[:, :KV_LORA], wv_ref[...]).astype(BF16)


def _kv_proj(h, g_src, w_down_aug, g_latent, cs_k, w_k, w_v):
    bsz, t, d = h.shape
    nt = t // TT
    full = lambda shape: pl.BlockSpec(shape, lambda b, i: (0,) * len(shape))
    return pl.pallas_call(
        _kv_kernel,
        grid=(bsz, nt),
        in_specs=[pl.BlockSpec((1, TT, d), lambda b, i: (b, i, 0)), full((1, d)), full(w_down_aug.shape),
                  full((1, KV_LORA)), pl.BlockSpec((TT, 2 * QK_ROPE), lambda b, i: (i, 0)),
                  full(w_k.shape), full(w_v.shape)],
        out_specs=[pl.BlockSpec((1, TT, N_HEADS * HEAD_PAD), lambda b, i: (b, i, 0)),
                   pl.BlockSpec((1, TT, N_HEADS * V_HEAD), lambda b, i: (b, i, 0))],
        out_shape=[jax.ShapeDtypeStruct((bsz, t, N_HEADS * HEAD_PAD), BF16),
                   jax.ShapeDtypeStruct((bsz, t, N_HEADS * V_HEAD), BF16)],
        compiler_params=_cparams(2),
        name="kv_proj",
    )(h, g_src, w_down_aug, g_latent, cs_k, w_k, w_v)


def _q_kernel(h_ref, g_ref, wdq_ref, gq_ref, wuq_ref, ct_ref, st_ref, q_ref):
    hn = _rms(h_ref[0], g_ref[...]).astype(BF16)
    c_q = _rms(_dot(hn, wdq_ref[...]), gq_ref[...]).astype(BF16)
    q = _dot(c_q, wuq_ref[...])
    width = q.shape[1]
    ct = jnp.tile(ct_ref[...], (1, N_HEADS))
    st = jnp.tile(st_ref[...], (1, N_HEADS))
    q_ref[0] = (q * ct + pltpu.roll(q, width - QK_ROPE, axis=1) * st).astype(BF16)


def _q_proj(h, g, w_dq, g_q, w_uq_aug, ctab, stab):
    bsz, t, d = h.shape
    nt = t // TT
    full = lambda shape: pl.BlockSpec(shape, lambda b, i: (0,) * len(shape))
    tab = pl.BlockSpec((TT, HEAD_PAD), lambda b, i: (i, 0))
    return pl.pallas_call(
        _q_kernel,
        grid=(bsz, nt),
        in_specs=[pl.BlockSpec((1, TT, d), lambda b, i: (b, i, 0)), full((1, d)), full(w_dq.shape),
                  full((1, Q_LORA)), full(w_uq_aug.shape), tab, tab],
        out_specs=pl.BlockSpec((1, TT, N_HEADS * HEAD_PAD), lambda b, i: (b, i, 0)),
        out_shape=jax.ShapeDtypeStruct((bsz, t, N_HEADS * HEAD_PAD), BF16),
        compiler_params=_cparams(2),
        name="q_proj",
    )(h, g, w_dq, g_q, w_uq_aug, ctab, stab)


def _attn_kernel(q_ref, k_ref, v_ref, o_ref, mb_ref, acc_ref):
    n_qblk = (q_ref.shape[1] - N_META) // Q_BLOCK
    heads = (0, 1)
    qk_sl = [slice(hh * HEAD_PAD, (hh + 1) * HEAD_PAD) for hh in heads]
    pair_v = 2 * V_HEAD

    def v_ext(vv):
        n = vv.shape[0]
        head0 = lax.broadcasted_iota(jnp.int32, (n, pair_v), 1) < V_HEAD
        sel0 = jnp.where(head0, 1.0, 0.0).astype(BF16)
        sel1 = jnp.where(head0, 0.0, 1.0).astype(BF16)
        r = lax.broadcasted_iota(jnp.int32, (2 * n, pair_v), 0)
        c = lax.broadcasted_iota(jnp.int32, (2 * n, pair_v), 1)
        ones = jnp.where(c == jnp.where(r < n, 0, 1), 1.0, 0.0).astype(BF16)
        return jnp.concatenate([jnp.concatenate([vv * sel0, vv * sel1], axis=0), ones], axis=1)

    def normalise(acc):
        head0 = lax.broadcasted_iota(jnp.int32, (acc.shape[0], pair_v), 1) < V_HEAD
        inv = jnp.where(head0, 1.0 / acc[:, pair_v:pair_v + 1], 1.0 / acc[:, pair_v + 1:pair_v + 2])
        return (acc[:, :pair_v] * inv).astype(BF16)

    k_meta = [k_ref[0, 0:N_META, qk_sl[hh]] for hh in heads]
    v_meta = v_ext(v_ref[0, 0:N_META, :])

    ps = []
    for hh in heads:
        s = _dot_nt(q_ref[0, 0:N_META, qk_sl[hh]], k_meta[hh])
        ps.append(jnp.exp(s - jnp.max(s, axis=1, keepdims=True)))
    o_ref[0, 0:N_META, :] = normalise(_dot(jnp.concatenate(ps, axis=1).astype(BF16), v_meta))

    rc = lax.broadcasted_iota(jnp.int32, (Q_BLOCK, Q_BLOCK), 0) // CHUNK
    cc = lax.broadcasted_iota(jnp.int32, (Q_BLOCK, Q_BLOCK), 1) // CHUNK
    diag_mask = cc <= rc

    def fold(s):
        tiles = [s[:, c:c + LANES] for c in range(0, s.shape[1], LANES)]
        return functools.reduce(jnp.maximum, tiles)

    def q_block(i, _):
        r0 = pl.multiple_of(N_META + i * Q_BLOCK, N_META)
        qs = [q_ref[0, pl.ds(r0, Q_BLOCK), qk_sl[hh]] for hh in heads]

        def scores(hh, c0):
            return _dot_nt(qs[hh], k_ref[0, pl.ds(c0, Q_BLOCK), qk_sl[hh]])

        def diag_scores(hh):
            return jnp.where(diag_mask, scores(hh, r0), MASK_VALUE)

        s_meta = [_dot_nt(qs[hh], k_meta[hh]) for hh in heads]

        def max_step(j, mf):
            c0 = pl.multiple_of(N_META + j * Q_BLOCK, N_META)
            return tuple(jnp.maximum(mf[hh], fold(scores(hh, c0))) for hh in heads)

        mf = lax.fori_loop(0, i, max_step, tuple(fold(diag_scores(hh)) for hh in heads))
        for hh in heads:
            m = jnp.maximum(jnp.max(mf[hh], axis=1, keepdims=True), jnp.max(s_meta[hh], axis=1, keepdims=True))
            mb_ref[hh] = jnp.broadcast_to(m, (Q_BLOCK, LANES))

        def probs(hh, s):
            mb = mb_ref[hh]
            return jnp.exp(s - jnp.concatenate([mb] * (Q_BLOCK // LANES), axis=1)).astype(BF16)

        p_meta = [jnp.exp(s_meta[hh] - mb_ref[hh][:, :N_META]).astype(BF16) for hh in heads]
        acc_ref[...] = (_dot(jnp.concatenate(p_meta, axis=1), v_meta)
                        + _dot(jnp.concatenate([probs(hh, diag_scores(hh)) for hh in heads], axis=1),
                               v_ext(v_ref[0, pl.ds(r0, Q_BLOCK), :])))

        def acc_step(j, carry):
            c0 = pl.multiple_of(N_META + j * Q_BLOCK, N_META)
            p = jnp.concatenate([probs(hh, scores(hh, c0)) for hh in heads], axis=1)
            acc_ref[...] += _dot(p, v_ext(v_ref[0, pl.ds(c0, Q_BLOCK), :]))
            return carry

        lax.fori_loop(0, i, acc_step, 0)
        o_ref[0, pl.ds(r0, Q_BLOCK), :] = normalise(acc_ref[...])
        return 0

    lax.fori_loop(0, n_qblk, q_block, 0)


def _attention(q, k, v):
    bsz, t, _ = q.shape
    qk_spec = pl.BlockSpec((1, t, 2 * HEAD_PAD), lambda b, p: (b, 0, p))
    v_spec = pl.BlockSpec((1, t, 2 * V_HEAD), lambda b, p: (b, 0, p))
    return pl.pallas_call(
        _attn_kernel,
        grid=(bsz, N_HEADS // 2),
        in_specs=[qk_spec, qk_spec, v_spec],
        out_specs=v_spec,
        out_shape=jax.ShapeDtypeStruct((bsz, t, N_HEADS * V_HEAD), BF16),
        scratch_shapes=[pltpu.VMEM((2, Q_BLOCK, LANES), F32), pltpu.VMEM((Q_BLOCK, 4 * V_HEAD), F32)],
        compiler_params=_cparams(2),
        name="attention",
    )(q, k, v)


def _oproj_kernel(h_ref, a_ref, w_ref, o_ref):
    o_ref[0] = h_ref[0] + _dot(a_ref[0], w_ref[...])


def _o_proj(h, attn, w_o):
    bsz, t, d = h.shape
    nt = t // TT
    tile = pl.BlockSpec((1, TT, d), lambda b, i: (b, i, 0))
    return pl.pallas_call(
        _oproj_kernel,
        grid=(bsz, nt),
        in_specs=[tile, pl.BlockSpec((1, TT, attn.shape[2]), lambda b, i: (b, i, 0)),
                  pl.BlockSpec(w_o.shape, lambda b, i: (0, 0))],
        out_specs=tile,
        out_shape=jax.ShapeDtypeStruct(h.shape, F32),
        compiler_params=_cparams(2),
        name="o_proj",
    )(h, attn, w_o)


def _final_kernel(h_ref, g_ref, o_ref):
    o_ref[0] = _rms(h_ref[0, N_META:, :], g_ref[...])


def _final_norm(h, g):
    bsz, t, d = h.shape
    return pl.pallas_call(
        _final_kernel,
        grid=(bsz,),
        in_specs=[pl.BlockSpec((1, t, d), lambda b: (b, 0, 0)), pl.BlockSpec((1, d), lambda b: (0, 0))],
        out_specs=pl.BlockSpec((1, t - N_META, d), lambda b: (b, 0, 0)),
        out_shape=jax.ShapeDtypeStruct((bsz, t - N_META, d), F32),
        compiler_params=_cparams(1),
        name="final_norm",
    )(h, g)


def _rot_cols(w):
    half = w.shape[1] // 2
    return jnp.concatenate([-w[:, half:], w[:, :half]], axis=1)


def _gate_pairs(w_a, w_i):
    def bd(w):
        z = jnp.zeros((RNN_BLOCK, RNN_BLOCK), w.dtype)
        return jnp.stack([jnp.block([[w[2 * p], z], [z, w[2 * p + 1]]]) for p in range(N_RNN_BLOCKS // 2)])
    return jnp.concatenate([bd(w_a), bd(w_i)], axis=2).astype(BF16)


def _q_up_aug(w_uq):
    w = w_uq.reshape(Q_LORA, N_HEADS, QK_NOPE + QK_ROPE)
    pe = w[:, :, QK_NOPE:]
    rot = jnp.concatenate([-pe[:, :, QK_ROPE // 2:], pe[:, :, :QK_ROPE // 2]], axis=2)
    return jnp.concatenate([w, rot], axis=2).reshape(Q_LORA, N_HEADS * HEAD_PAD).astype(BF16)


def _kv_up_aug(w_up):
    w = w_up.reshape(KV_LORA, N_HEADS, QK_NOPE + V_HEAD)
    pad = HEAD_PAD - QK_NOPE
    w_k_top = jnp.concatenate([w[:, :, :QK_NOPE], jnp.zeros((KV_LORA, N_HEADS, pad), w.dtype)], axis=2)
    eye = jnp.concatenate([jnp.zeros((QK_ROPE, QK_NOPE), w.dtype), jnp.eye(QK_ROPE, dtype=w.dtype),
                           jnp.zeros((QK_ROPE, pad - QK_ROPE), w.dtype)], axis=1)
    w_k_bot = jnp.broadcast_to(eye[:, None, :], (QK_ROPE, N_HEADS, HEAD_PAD))
    w_k = jnp.concatenate([w_k_top, w_k_bot], axis=0).reshape(KV_LORA + QK_ROPE, N_HEADS * HEAD_PAD)
    w_v = w[:, :, QK_NOPE:].reshape(KV_LORA, N_HEADS * V_HEAD)
    return w_k.astype(BF16), w_v.astype(BF16)


def _rope_tables(t):
    inv_freq = ROPE_THETA ** (-jnp.arange(0, QK_ROPE, 2, dtype=F32) / QK_ROPE)
    ang = jnp.arange(t, dtype=F32)[:, None] * inv_freq[None, :]
    cos = jnp.tile(jnp.cos(ang), (1, 2))
    sin = jnp.tile(jnp.sin(ang), (1, 2))
    cs_k = jnp.concatenate([cos, sin], axis=1)
    ones = jnp.ones((t, QK_NOPE), F32)
    zeros = jnp.zeros((t, HEAD_PAD - QK_NOPE - QK_ROPE), F32)
    ctab = ATTN_SCALE * jnp.concatenate([ones, cos, zeros], axis=1)
    stab = ATTN_SCALE * jnp.concatenate([0.0 * ones, sin, zeros], axis=1)
    return cs_k, ctab, stab


def kernel(x, meta_tokens, norm_mix, norm_ffn, norm_final, rnn_w_in, rnn_conv_w, rnn_conv_b, rnn_w_a, rnn_b_a, rnn_w_i, rnn_b_i, rnn_lambda, rnn_w_out, kv_norm_src, kv_w_down, kv_latent_norm, kv_w_up, q_w_down, q_latent_norm, q_w_up, attn_w_out, ffn_w1, ffn_w3, ffn_w2, moe_router, moe_w1, moe_w3, moe_w2):
    bsz = x.shape[0]
    meta = jnp.broadcast_to(meta_tokens.astype(x.dtype)[None], (bsz, N_META, x.shape[-1]))
    h = jnp.concatenate([meta, x], axis=1)
    t = h.shape[1]
    assert t % TT == 0 and (t - N_META) % Q_BLOCK == 0
    row = lambda v: v.reshape(1, -1)
    cs_k, ctab, stab = _rope_tables(t)
    no_router = jnp.zeros((D_MODEL, 1), F32)

    k = v = None
    for layer in range(DEPTH):
        if layer < N_A_LAYERS:
            a = layer
            h = _rnn_layer(h, row(norm_mix[layer]), rnn_w_in[a].astype(BF16), rnn_conv_w[a], row(rnn_conv_b[a]),
                           _gate_pairs(rnn_w_a[a], rnn_w_i[a]), row(rnn_b_a[a]), row(rnn_b_i[a]),
                           row(rnn_lambda[a]), rnn_w_out[a].astype(BF16))
        else:
            b = layer - N_A_LAYERS
            if b == 0:
                w_down_aug = jnp.concatenate([kv_w_down, _rot_cols(kv_w_down[:, KV_LORA:])], axis=1).astype(BF16)
                w_k, w_v = _kv_up_aug(kv_w_up)
                k, v = _kv_proj(h, row(kv_norm_src), w_down_aug, row(kv_latent_norm), cs_k, w_k, w_v)
            q = _q_proj(h, row(norm_mix[layer]), q_w_down[b].astype(BF16), row(q_latent_norm[b]),
                        _q_up_aug(q_w_up[b]), ctab, stab)
            h = _o_proj(h, _attention(q, k, v), attn_w_out[b].astype(BF16))
        if layer % 2 == 0:
            e = layer // 2
            h = _ffn_layer(h, row(norm_ffn[layer]), no_router, ffn_w1[e:e + 1].astype(BF16),
                           ffn_w3[e:e + 1].astype(BF16), ffn_w2[e:e + 1].astype(BF16), tf=1408)
        else:
            m = layer // 2
            h = _ffn_layer(h, row(norm_ffn[layer]), moe_router[m], moe_w1[m].astype(BF16),
                           moe_w3[m].astype(BF16), moe_w2[m].astype(BF16), tf=1792)
    return _final_norm(h, row(norm_final))
```

```python
import functools
import math

import jax
import jax.numpy as jnp
from jax import lax
from jax.experimental import pallas as pl
from jax.experimental.pallas import tpu as pltpu

D_MODEL = 1024
N_META = 16
CHUNK = 64
NORM_EPS = 1e-6
DEPTH = 4
N_A_LAYERS = DEPTH // 2

D_RNN = D_MODEL
N_RNN_BLOCKS = 8
RNN_BLOCK = D_RNN // N_RNN_BLOCKS
CONV_WIDTH = 4
LRU_C = 8.0

N_HEADS = 16
QK_NOPE = 64
QK_ROPE = 32
V_HEAD = 64
Q_LORA = 384
KV_LORA = 256
ROPE_THETA = 10000.0
ATTN_SCALE = 1.0 / math.sqrt(QK_NOPE + QK_ROPE)
MASK_VALUE = -1e30

N_EXPERTS = 8

LANES = 128
SUBLANES = 8
VMEM_LIMIT = 56 * 1024 * 1024

HEAD_PAD = 128
Q_BLOCK = 512
TT = 688
TM = 512
SEG_ALIGN = 16
SEG_SIZES = (512, 256, 128, 64, 32, 16)
TILE_ROWS = 2 * TM + N_EXPERTS * SEG_ALIGN
F32 = jnp.float32
BF16 = jnp.bfloat16


def _cparams(n_axes):
    return pltpu.CompilerParams(dimension_semantics=("arbitrary",) * n_axes,
                                vmem_limit_bytes=VMEM_LIMIT)


def _rms(x, g):
    return x * lax.rsqrt(jnp.mean(x * x, axis=-1, keepdims=True) + NORM_EPS) * g


def _dot(a, b):
    return jnp.dot(a, b, preferred_element_type=F32)


def _dot_nt(a, b):
    return lax.dot_general(a, b, (((1,), (1,)), ((), ())), preferred_element_type=F32)


def _rnn_kernel(h_ref, g_ref, win_ref, cw_ref, cb_ref, wg_ref, ba_ref, bi_ref, lam_ref, wout_ref,
                o_ref, xpad_ref, yg_ref, a_ref, u_ref, carry_ref):
    t = pl.program_id(1)
    tt = h_ref.shape[1]

    @pl.when(t == 0)
    def _():
        xpad_ref[0:SUBLANES, :] = jnp.zeros((SUBLANES, D_RNN), F32)
        carry_ref[...] = jnp.zeros_like(carry_ref)

    x = h_ref[0]
    hn = _rms(x, g_ref[...]).astype(BF16)
    yg_ref[...] = jax.nn.gelu(_dot(hn, win_ref[:, :D_RNN]), approximate=True)
    xpad_ref[SUBLANES:, :] = _dot(hn, win_ref[:, D_RNN:])

    xc = cb_ref[...] + cw_ref[CONV_WIDTH - 1:CONV_WIDTH, :] * xpad_ref[SUBLANES:, :]
    for j in range(CONV_WIDTH - 1):
        off = SUBLANES - (CONV_WIDTH - 1) + j
        xc = xc + cw_ref[j:j + 1, :] * xpad_ref[off:off + tt, :]
    xpad_ref[0:SUBLANES, :] = xpad_ref[tt:tt + SUBLANES, :]

    xcb = xc.astype(BF16)
    log_coef = -LRU_C * jax.nn.softplus(-lam_ref[...])
    pair = 2 * RNN_BLOCK
    for p in range(N_RNN_BLOCKS // 2):
        sl = slice(p * pair, (p + 1) * pair)
        gates = _dot(xcb[:, sl], wg_ref[p])
        gate_r = jax.nn.sigmoid(gates[:, :pair] + ba_ref[:, sl])
        gate_i = jax.nn.sigmoid(gates[:, pair:] + bi_ref[:, sl])
        log_a = log_coef[:, sl] * gate_r
        a_ref[:, sl] = jnp.exp(log_a)
        th = jnp.tanh(log_a)
        u_ref[:, sl] = jnp.sqrt(-2.0 * th / (1.0 - th)) * (gate_i * xc[:, sl])

    row = lax.broadcasted_iota(jnp.int32, (SUBLANES, D_RNN), 0)

    def group(i, carry):
        r0 = pl.multiple_of(i * SUBLANES, SUBLANES)
        a = a_ref[pl.ds(r0, SUBLANES), :]
        u = u_ref[pl.ds(r0, SUBLANES), :]
        for s in (1, 2, 4):
            keep = row >= s
            u = jnp.where(keep, a * pltpu.roll(u, s, axis=0) + u, u)
            a = jnp.where(keep, a * pltpu.roll(a, s, axis=0), a)
        hs = a * carry + u
        u_ref[pl.ds(r0, SUBLANES), :] = hs
        return hs[SUBLANES - 1:SUBLANES, :]

    carry_ref[...] = lax.fori_loop(0, tt // SUBLANES, group, carry_ref[...])

    y = (u_ref[...] * yg_ref[...]).astype(BF16)
    o_ref[0] = x + _dot(y, wout_ref[...])


def _rnn_layer(h, g, w_in, conv_w, conv_b, w_gate, b_a, b_i, lam, w_out):
    bsz, t, d = h.shape
    nt = t // TT
    full = lambda shape: pl.BlockSpec(shape, lambda b, i: (0,) * len(shape))
    tile = pl.BlockSpec((1, TT, d), lambda b, i: (b, i, 0))
    return pl.pallas_call(
        _rnn_kernel,
        grid=(bsz, nt),
        in_specs=[tile, full((1, d)), full((d, 2 * D_RNN)), full((CONV_WIDTH, D_RNN)), full((1, D_RNN)),
                  full(w_gate.shape), full((1, D_RNN)), full((1, D_RNN)), full((1, D_RNN)),
                  full((D_RNN, d))],
        out_specs=tile,
        out_shape=jax.ShapeDtypeStruct(h.shape, F32),
        scratch_shapes=[pltpu.VMEM((TT + SUBLANES, D_RNN), F32), pltpu.VMEM((TT, D_RNN), F32),
                        pltpu.VMEM((TT, D_RNN), F32), pltpu.VMEM((TT, D_RNN), F32),
                        pltpu.VMEM((1, D_RNN), F32)],
        compiler_params=_cparams(2),
        name="rnn_mixer",
    )(h, g, w_in, conv_w, conv_b, w_gate, b_a, b_i, lam, w_out)


def _ffn_kernel(h_ref, g_ref, w1_ref, w3_ref, w2_ref, o_ref, hf_ref, acc_ref):
    f = pl.program_id(2)

    @pl.when(f == 0)
    def _():
        hf_ref[...] = _rms(h_ref[0], g_ref[...]).astype(BF16)
        acc_ref[...] = jnp.zeros_like(acc_ref)

    hf = hf_ref[...]
    act = jax.nn.silu(_dot(hf, w1_ref[...])) * _dot(hf, w3_ref[...])
    acc_ref[...] += _dot(act.astype(BF16), w2_ref[...])

    @pl.when(f == pl.num_programs(2) - 1)
    def _():
        o_ref[0] = h_ref[0] + acc_ref[...]


def _ffn_layer(h, g, w1, w3, w2, tf):
    bsz, t, d = h.shape
    d_ff = w1.shape[1]
    nt, nf = t // TT, d_ff // tf
    tile = pl.BlockSpec((1, TT, d), lambda b, i, f: (b, i, 0))
    return pl.pallas_call(
        _ffn_kernel,
        grid=(bsz, nt, nf),
        in_specs=[tile,
                  pl.BlockSpec((1, d), lambda b, i, f: (0, 0)),
                  pl.BlockSpec((d, tf), lambda b, i, f: (0, f)),
                  pl.BlockSpec((d, tf), lambda b, i, f: (0, f)),
                  pl.BlockSpec((tf, d), lambda b, i, f: (f, 0))],
        out_specs=tile,
        out_shape=jax.ShapeDtypeStruct(h.shape, F32),
        scratch_shapes=[pltpu.VMEM((TT, d), BF16), pltpu.VMEM((TT, d), F32)],
        compiler_params=_cparams(3),
        name="ffn_dense",
    )(h, g, w1, w3, w2)


def _router_kernel(h_ref, g_ref, wrt_ref, hf_ref, route_ref, cnt_ref):
    hf = _rms(h_ref[...], g_ref[...])
    hf_ref[...] = hf.astype(BF16)
    logits = lax.dot_general(wrt_ref[...], hf, (((1,), (1,)), ((), ())), preferred_element_type=F32,
                             precision=lax.Precision.HIGHEST)
    eidx = lax.broadcasted_iota(jnp.int32, logits.shape, 0)
    m1 = jnp.max(logits, axis=0, keepdims=True)
    i1 = jnp.min(jnp.where(logits == m1, eidx, N_EXPERTS), axis=0, keepdims=True)
    rest = jnp.where(eidx == i1, -jnp.inf, logits)
    m2 = jnp.max(rest, axis=0, keepdims=True)
    i2 = jnp.min(jnp.where(rest == m2, eidx, N_EXPERTS), axis=0, keepdims=True)
    ex = jnp.exp(m2 - m1)
    inv = 1.0 / (1.0 + ex)
    sel = jnp.where(eidx == i1, 1.0, 0.0) + jnp.where(eidx == i2, 1.0, 0.0)
    cnt_ref[0] = jnp.broadcast_to(jnp.sum(sel, axis=1, keepdims=True), cnt_ref.shape[1:])
    route_ref[0] = jnp.where(eidx == 0, i1.astype(F32), jnp.where(eidx == 1, i2.astype(F32),
                             jnp.where(eidx == 2, inv, jnp.where(eidx == 3, ex * inv, 0.0))))


def _router(h2d, g, w_router_t):
    m, d = h2d.shape
    nt = m // TM
    return pl.pallas_call(
        _router_kernel,
        grid=(nt,),
        in_specs=[pl.BlockSpec((TM, d), lambda t: (t, 0)), pl.BlockSpec((1, d), lambda t: (0, 0)),
                  pl.BlockSpec((N_EXPERTS, d), lambda t: (0, 0))],
        out_specs=[pl.BlockSpec((TM, d), lambda t: (t, 0)),
                   pl.BlockSpec((1, N_EXPERTS, TM), lambda t: (t, 0, 0)),
                   pl.BlockSpec((1, N_EXPERTS, LANES), lambda t: (t, 0, 0))],
        out_shape=[jax.ShapeDtypeStruct((m, d), BF16),
                   jax.ShapeDtypeStruct((nt, N_EXPERTS, TM), F32),
                   jax.ShapeDtypeStruct((nt, N_EXPERTS, LANES), F32)],
        compiler_params=_cparams(1),
        name="moe_router",
    )(h2d, g, w_router_t)


def _segment_copies(t, seg_ref, len_ref, base_ref, make_copy, start):
    for e in range(N_EXPERTS):
        n = len_ref[t * N_EXPERTS + e]
        local0 = seg_ref[t * N_EXPERTS + e]
        global0 = base_ref[t * N_EXPERTS + e]
        off = 0
        for p in SEG_SIZES:
            take = (n & p) != 0

            @pl.when(take)
            def _(off=off, p=p):
                cp = make_copy(pl.multiple_of(local0 + off, SEG_ALIGN), pl.multiple_of(global0 + off, SEG_ALIGN), p)
                if start:
                    cp.start()
                else:
                    cp.wait()

            off = off + jnp.where(take, p, 0)


def _dispatch_kernel(seg_ref, len_ref, base_ref, hf_ref, route_ref, tri_ref, xin_ref, xs_ref, slot_ref,
                     stage_ref, sem):
    del xin_ref
    t = pl.program_id(0)
    route = route_ref[0]
    eidx = lax.broadcasted_iota(jnp.int32, route.shape, 0)
    sel1 = eidx == route[0:1, :].astype(jnp.int32)
    sel2 = eidx == route[1:2, :].astype(jnp.int32)
    sel = jnp.where(sel1, 1.0, 0.0) + jnp.where(sel2, 1.0, 0.0)
    sel16 = jnp.concatenate([sel, jnp.zeros_like(sel)], axis=0).astype(BF16)
    rank = _dot(sel16, tri_ref[...])[:N_EXPERTS]
    seg = jnp.zeros(route.shape, jnp.int32)
    for e in range(N_EXPERTS):
        seg = jnp.where(eidx == e, seg_ref[t * N_EXPERTS + e], seg)
    slot = seg.astype(F32) + rank
    slot1 = jnp.sum(jnp.where(sel1, slot, 0.0), axis=0, keepdims=True)
    slot2 = jnp.sum(jnp.where(sel2, slot, 0.0), axis=0, keepdims=True)
    slot_ref[0] = jnp.where(eidx == 0, slot1, jnp.where(eidx == 1, slot2, 0.0))
    r = lax.broadcasted_iota(jnp.int32, (TILE_ROWS, TM), 0)
    perm = jnp.where(r == slot1.astype(jnp.int32), 1.0, jnp.where(r == slot2.astype(jnp.int32), 1.0, 0.0))
    stage_ref[...] = _dot(perm.astype(BF16), hf_ref[...]).astype(BF16)

    def copy(local, glob, p):
        return pltpu.make_async_copy(stage_ref.at[pl.ds(local, p)], xs_ref.at[pl.ds(glob, p)], sem)

    _segment_copies(t, seg_ref, len_ref, base_ref, copy, start=True)
    _segment_copies(t, seg_ref, len_ref, base_ref, copy, start=False)


def _dispatch(seg, seg_len, base, hf, route, tri, x_zero):
    m, d = hf.shape
    nt = m // TM
    grid_spec = pltpu.PrefetchScalarGridSpec(
        num_scalar_prefetch=3,
        grid=(nt,),
        in_specs=[pl.BlockSpec((TM, d), lambda t, *_: (t, 0)),
                  pl.BlockSpec((1, N_EXPERTS, TM), lambda t, *_: (t, 0, 0)),
                  pl.BlockSpec((TM, TM), lambda t, *_: (0, 0)),
                  pl.BlockSpec(memory_space=pl.ANY)],
        out_specs=[pl.BlockSpec(memory_space=pl.ANY),
                   pl.BlockSpec((1, N_EXPERTS, TM), lambda t, *_: (t, 0, 0))],
        scratch_shapes=[pltpu.VMEM((TILE_ROWS, d), BF16), pltpu.SemaphoreType.DMA(())],
    )
    return pl.pallas_call(
        _dispatch_kernel,
        grid_spec=grid_spec,
        out_shape=[jax.ShapeDtypeStruct(x_zero.shape, BF16),
                   jax.ShapeDtypeStruct((nt, N_EXPERTS, TM), F32)],
        input_output_aliases={6: 0},
        compiler_params=_cparams(1),
        name="moe_dispatch",
    )(seg, seg_len, base, hf, route, tri, x_zero)


def _experts_kernel(te_ref, na_ref, x_ref, w1_ref, w3_ref, w2_ref, o_ref, acc_ref):
    del te_ref
    i = pl.program_id(0)
    f = pl.program_id(1)

    @pl.when(i < na_ref[0])
    def _():
        @pl.when(f == 0)
        def _():
            acc_ref[...] = jnp.zeros_like(acc_ref)

        x = x_ref[...]
        act = jax.nn.silu(_dot(x, w1_ref[0])) * _dot(x, w3_ref[0])
        acc_ref[...] += _dot(act.astype(BF16), w2_ref[0])

        @pl.when(f == pl.num_programs(1) - 1)
        def _():
            o_ref[...] = acc_ref[...].astype(BF16)

    @pl.when(jnp.logical_and(i >= na_ref[0], f == pl.num_programs(1) - 1))
    def _():
        o_ref[...] = jnp.zeros_like(o_ref)


def _experts(tile_expert, n_active, x_sorted, w1, w3, w2, tf):
    rows, d = x_sorted.shape
    d_ff = w1.shape[2]
    n_tiles, nf = rows // TM, d_ff // tf
    row_blk = lambda i, f, te, na: (jnp.minimum(i, na[0] - 1), 0)
    chunk = lambda i, f, na: jnp.where(i < na[0], f, nf - 1)
    grid_spec = pltpu.PrefetchScalarGridSpec(
        num_scalar_prefetch=2,
        grid=(n_tiles, nf),
        in_specs=[pl.BlockSpec((TM, d), row_blk),
                  pl.BlockSpec((1, d, tf), lambda i, f, te, na: (te[i], 0, chunk(i, f, na))),
                  pl.BlockSpec((1, d, tf), lambda i, f, te, na: (te[i], 0, chunk(i, f, na))),
                  pl.BlockSpec((1, tf, d), lambda i, f, te, na: (te[i], chunk(i, f, na), 0))],
        out_specs=pl.BlockSpec((TM, d), lambda i, f, te, na: (i, 0)),
        scratch_shapes=[pltpu.VMEM((TM, d), F32)],
    )
    return pl.pallas_call(
        _experts_kernel,
        grid_spec=grid_spec,
        out_shape=jax.ShapeDtypeStruct((rows, d), BF16),
        compiler_params=_cparams(2),
        name="moe_experts",
    )(tile_expert, n_active, x_sorted, w1, w3, w2)


def _combine_kernel(seg_ref, len_ref, base_ref, h_ref, slot_ref, gate_ref, y_ref, o_ref, ybuf_ref, sem):
    t = pl.program_id(0)
    ybuf_ref[2 * TM:, :] = jnp.zeros((TILE_ROWS - 2 * TM, ybuf_ref.shape[1]), BF16)

    def copy(local, glob, p):
        return pltpu.make_async_copy(y_ref.at[pl.ds(glob, p)], ybuf_ref.at[pl.ds(local, p)], sem)

    _segment_copies(t, seg_ref, len_ref, base_ref, copy, start=True)
    _segment_copies(t, seg_ref, len_ref, base_ref, copy, start=False)

    y = ybuf_ref[...]
    lane = lax.broadcasted_iota(jnp.int32, (TM, TILE_ROWS), 1)
    out = h_ref[...]
    for k in range(2):
        onehot = jnp.where(lane == slot_ref[:, k:k + 1], 1.0, 0.0).astype(BF16)
        out = out + gate_ref[:, k:k + 1] * _dot(onehot, y)
    o_ref[...] = out


def _combine(seg, seg_len, base, h2d, slots, gates, y_sorted):
    m, d = h2d.shape
    nt = m // TM
    grid_spec = pltpu.PrefetchScalarGridSpec(
        num_scalar_prefetch=3,
        grid=(nt,),
        in_specs=[pl.BlockSpec((TM, d), lambda t, *_: (t, 0)),
                  pl.BlockSpec((TM, 2), lambda t, *_: (t, 0)),
                  pl.BlockSpec((TM, 2), lambda t, *_: (t, 0)),
                  pl.BlockSpec(memory_space=pl.ANY)],
        out_specs=pl.BlockSpec((TM, d), lambda t, *_: (t, 0)),
        scratch_shapes=[pltpu.VMEM((TILE_ROWS, d), BF16), pltpu.SemaphoreType.DMA(())],
    )
    return pl.pallas_call(
        _combine_kernel,
        grid_spec=grid_spec,
        out_shape=jax.ShapeDtypeStruct((m, d), F32),
        compiler_params=_cparams(1),
        name="moe_combine",
    )(seg, seg_len, base, h2d, slots, gates, y_sorted)


def _moe_tables(counts, n_tiles_max):
    pad = (counts + SEG_ALIGN - 1) // SEG_ALIGN * SEG_ALIGN
    seg = jnp.cumsum(pad, axis=1) - pad
    region = (jnp.sum(pad, axis=0) + TM - 1) // TM * TM
    base = (jnp.cumsum(region) - region)[None, :] + jnp.cumsum(pad, axis=0) - pad
    tiles_end = jnp.cumsum(region // TM)
    n_active = tiles_end[-1]
    tile = jnp.minimum(jnp.arange(n_tiles_max), n_active - 1)
    tile_expert = jnp.sum(tile[:, None] >= tiles_end[None, :], axis=1)
    flat = lambda a: a.reshape(-1).astype(jnp.int32)
    return flat(seg), flat(pad), flat(base), flat(tile_expert), flat(n_active)


def _moe_layer(h, g, w_router, w1, w3, w2, tf):
    bsz, t, d = h.shape
    m = bsz * t
    nt = m // TM
    n_tiles_max = -(-(nt * (TILE_ROWS - SEG_ALIGN) + N_EXPERTS * (TM - 1)) // TM)
    h2d = h.reshape(m, d)
    hf, route, counts = _router(h2d, g, w_router.T)
    seg, seg_len, base, tile_expert, n_active = _moe_tables(counts[:, :, 0].astype(jnp.int32), n_tiles_max)
    tri = (jnp.arange(TM)[:, None] < jnp.arange(TM)[None, :]).astype(BF16)
    x_sorted, slots = _dispatch(seg, seg_len, base, hf, route, tri, jnp.zeros((n_tiles_max * TM, d), BF16))
    y_sorted = _experts(tile_expert, n_active, x_sorted, w1, w3, w2, tf)
    to_cols = lambda a: a.transpose(0, 2, 1).reshape(m, 2)
    out = _combine(seg, seg_len, base, h2d, to_cols(slots[:, :2, :]).astype(jnp.int32),
                   to_cols(route[:, 2:4, :]), y_sorted)
    return out.reshape(bsz, t, d)


def _kv_kernel(h_ref, g_ref, wd_ref, gl_ref, cs_ref, wk_ref, wv_ref, k_ref, v_ref):
    hs = _rms(h_ref[0], g_ref[...]).astype(BF16)
    ckv = _dot(hs, wd_ref[...])
    c_lat = _rms(ckv[:, :KV_LORA], gl_ref[...])
    pe = ckv[:, KV_LORA:KV_LORA + QK_ROPE]
    pe_rot = ckv[:, KV_LORA + QK_ROPE:]
    k_pe = pe * cs_ref[:, :QK_ROPE] + pe_rot * cs_ref[:, QK_ROPE:]
    lat = jnp.concatenate([c_lat, k_pe], axis=1).astype(BF16)
    k_ref[0] = _dot(lat, wk_ref[...]).astype(BF16)
    v_ref[0] = _dot(lat[:, :KV_LORA], wv_ref[...]).astype(BF16)


def _kv_proj(h, g_src, w_down_aug, g_latent, cs_k, w_k, w_v):
    bsz, t, d = h.shape
    nt = t // TT
    full = lambda shape: pl.BlockSpec(shape, lambda b, i: (0,) * len(shape))
    return pl.pallas_call(
        _kv_kernel,
        grid=(bsz, nt),
        in_specs=[pl.BlockSpec((1, TT, d), lambda b, i: (b, i, 0)), full((1, d)), full(w_down_aug.shape),
                  full((1, KV_LORA)), pl.BlockSpec((TT, 2 * QK_ROPE), lambda b, i: (i, 0)),
                  full(w_k.shape), full(w_v.shape)],
        out_specs=[pl.BlockSpec((1, TT, N_HEADS * HEAD_PAD), lambda b, i: (b, i, 0)),
                   pl.BlockSpec((1, TT, N_HEADS * V_HEAD), lambda b, i: (b, i, 0))],
        out_shape=[jax.ShapeDtypeStruct((bsz, t, N_HEADS * HEAD_PAD), BF16),
                   jax.ShapeDtypeStruct((bsz, t, N_HEADS * V_HEAD), BF16)],
        compiler_params=_cparams(2),
        name="kv_proj",
    )(h, g_src, w_down_aug, g_latent, cs_k, w_k, w_v)


def _q_kernel(h_ref, g_ref, wdq_ref, gq_ref, wuq_ref, ct_ref, st_ref, q_ref):
    hn = _rms(h_ref[0], g_ref[...]).astype(BF16)
    c_q = _rms(_dot(hn, wdq_ref[...]), gq_ref[...]).astype(BF16)
    q = _dot(c_q, wuq_ref[...])
    width = q.shape[1]
    ct = jnp.tile(ct_ref[...], (1, N_HEADS))
    st = jnp.tile(st_ref[...], (1, N_HEADS))
    q_ref[0] = (q * ct + pltpu.roll(q, width - QK_ROPE, axis=1) * st).astype(BF16)


def _q_proj(h, g, w_dq, g_q, w_uq_aug, ctab, stab):
    bsz, t, d = h.shape
    nt = t // TT
    full = lambda shape: pl.BlockSpec(shape, lambda b, i: (0,) * len(shape))
    tab = pl.BlockSpec((TT, HEAD_PAD), lambda b, i: (i, 0))
    return pl.pallas_call(
        _q_kernel,
        grid=(bsz, nt),
        in_specs=[pl.BlockSpec((1, TT, d), lambda b, i: (b, i, 0)), full((1, d)), full(w_dq.shape),
                  full((1, Q_LORA)), full(w_uq_aug.shape), tab, tab],
        out_specs=pl.BlockSpec((1, TT, N_HEADS * HEAD_PAD), lambda b, i: (b, i, 0)),
        out_shape=jax.ShapeDtypeStruct((bsz, t, N_HEADS * HEAD_PAD), BF16),
        compiler_params=_cparams(2),
        name="q_proj",
    )(h, g, w_dq, g_q, w_uq_aug, ctab, stab)


def _attn_kernel(q_ref, k_ref, v_ref, o_ref, mb_ref, acc_ref):
    n_qblk = (q_ref.shape[1] - N_META) // Q_BLOCK
    heads = (0, 1)
    qk_sl = [slice(hh * HEAD_PAD, (hh + 1) * HEAD_PAD) for hh in heads]
    pair_v = 2 * V_HEAD

    def v_ext(vv):
        n = vv.shape[0]
        head0 = lax.broadcasted_iota(jnp.int32, (n, pair_v), 1) < V_HEAD
        sel0 = jnp.where(head0, 1.0, 0.0).astype(BF16)
        sel1 = jnp.where(head0, 0.0, 1.0).astype(BF16)
        r = lax.broadcasted_iota(jnp.int32, (2 * n, pair_v), 0)
        c = lax.broadcasted_iota(jnp.int32, (2 * n, pair_v), 1)
        ones = jnp.where(c == jnp.where(r < n, 0, 1), 1.0, 0.0).astype(BF16)
        return jnp.concatenate([jnp.concatenate([vv * sel0, vv * sel1], axis=0), ones], axis=1)

    def normalise(acc):
        head0 = lax.broadcasted_iota(jnp.int32, (acc.shape[0], pair_v), 1) < V_HEAD
        inv = jnp.where(head0, 1.0 / acc[:, pair_v:pair_v + 1], 1.0 / acc[:, pair_v + 1:pair_v + 2])
        return (acc[:, :pair_v] * inv).astype(BF16)

    k_meta = [k_ref[0, 0:N_META, qk_sl[hh]] for hh in heads]
    v_meta = v_ext(v_ref[0, 0:N_META, :])

    ps = []
    for hh in heads:
        s = _dot_nt(q_ref[0, 0:N_META, qk_sl[hh]], k_meta[hh])
        ps.append(jnp.exp(s - jnp.max(s, axis=1, keepdims=True)))
    o_ref[0, 0:N_META, :] = normalise(_dot(jnp.concatenate(ps, axis=1).astype(BF16), v_meta))

    rc = lax.broadcasted_iota(jnp.int32, (Q_BLOCK, Q_BLOCK), 0) // CHUNK
    cc = lax.broadcasted_iota(jnp.int32, (Q_BLOCK, Q_BLOCK), 1) // CHUNK
    diag_mask = cc <= rc

    def fold(s):
        tiles = [s[:, c:c + LANES] for c in range(0, s.shape[1], LANES)]
        return functools.reduce(jnp.maximum, tiles)

    def q_block(i, _):
        r0 = pl.multiple_of(N_META + i * Q_BLOCK, N_META)
        qs = [q_ref[0, pl.ds(r0, Q_BLOCK), qk_sl[hh]] for hh in heads]

        def scores(hh, c0):
            return _dot_nt(qs[hh], k_ref[0, pl.ds(c0, Q_BLOCK), qk_sl[hh]])

        def diag_scores(hh):
            return jnp.where(diag_mask, scores(hh, r0), MASK_VALUE)

        s_meta = [_dot_nt(qs[hh], k_meta[hh]) for hh in heads]

        def max_step(j, mf):
            c0 = pl.multiple_of(N_META + j * Q_BLOCK, N_META)
            return tuple(jnp.maximum(mf[hh], fold(scores(hh, c0))) for hh in heads)

        mf = lax.fori_loop(0, i, max_step, tuple(fold(diag_scores(hh)) for hh in heads))
        for hh in heads:
            m = jnp.maximum(jnp.max(mf[hh], axis=1, keepdims=True), jnp.max(s_meta[hh], axis=1, keepdims=True))
            mb_ref[hh] = jnp.broadcast_to(m, (Q_BLOCK, LANES))

        def probs(hh, s):
            mb = mb_ref[hh]
            return jnp.exp(s - jnp.concatenate([mb] * (Q_BLOCK // LANES), axis=1)).astype(BF16)

        p_meta = [jnp.exp(s_meta[hh] - mb_ref[hh][:, :N_META]).astype(BF16) for hh in heads]
        acc_ref[...] = (_dot(jnp.concatenate(p_meta, axis=1), v_meta)
                        + _dot(jnp.concatenate([probs(hh, diag_scores(hh)) for hh in heads], axis=1),
                               v_ext(v_ref[0, pl.ds(r0, Q_BLOCK), :])))

        def acc_step(j, carry):
            c0 = pl.multiple_of(N_META + j * Q_BLOCK, N_META)
            p = jnp.concatenate([probs(hh, scores(hh, c0)) for hh in heads], axis=1)
            acc_ref[...] += _dot(p, v_ext(v_ref[0, pl.ds(c0, Q_BLOCK), :]))
            return carry

        lax.fori_loop(0, i, acc_step, 0)
        o_ref[0, pl.ds(r0, Q_BLOCK), :] = normalise(acc_ref[...])
        return 0

    lax.fori_loop(0, n_qblk, q_block, 0)


def _attention(q, k, v):
    bsz, t, _ = q.shape
    qk_spec = pl.BlockSpec((1, t, 2 * HEAD_PAD), lambda b, p: (b, 0, p))
    v_spec = pl.BlockSpec((1, t, 2 * V_HEAD), lambda b, p: (b, 0, p))
    return pl.pallas_call(
        _attn_kernel,
        grid=(bsz, N_HEADS // 2),
        in_specs=[qk_spec, qk_spec, v_spec],
        out_specs=v_spec,
        out_shape=jax.ShapeDtypeStruct((bsz, t, N_HEADS * V_HEAD), BF16),
        scratch_shapes=[pltpu.VMEM((2, Q_BLOCK, LANES), F32), pltpu.VMEM((Q_BLOCK, 4 * V_HEAD), F32)],
        compiler_params=_cparams(2),
        name="attention",
    )(q, k, v)


def _oproj_kernel(h_ref, a_ref, w_ref, o_ref):
    o_ref[0] = h_ref[0] + _dot(a_ref[0], w_ref[...])


def _o_proj(h, attn, w_o):
    bsz, t, d = h.shape
    nt = t // TT
    tile = pl.BlockSpec((1, TT, d), lambda b, i: (b, i, 0))
    return pl.pallas_call(
        _oproj_kernel,
        grid=(bsz, nt),
        in_specs=[tile, pl.BlockSpec((1, TT, attn.shape[2]), lambda b, i: (b, i, 0)),
                  pl.BlockSpec(w_o.shape, lambda b, i: (0, 0))],
        out_specs=tile,
        out_shape=jax.ShapeDtypeStruct(h.shape, F32),
        compiler_params=_cparams(2),
        name="o_proj",
    )(h, attn, w_o)


def _final_kernel(h_ref, g_ref, o_ref):
    o_ref[0] = _rms(h_ref[0, N_META:, :], g_ref[...])


def _final_norm(h, g):
    bsz, t, d = h.shape
    return pl.pallas_call(
        _final_kernel,
        grid=(bsz,),
        in_specs=[pl.BlockSpec((1, t, d), lambda b: (b, 0, 0)), pl.BlockSpec((1, d), lambda b: (0, 0))],
        out_specs=pl.BlockSpec((1, t - N_META, d), lambda b: (b, 0, 0)),
        out_shape=jax.ShapeDtypeStruct((bsz, t - N_META, d), F32),
        compiler_params=_cparams(1),
        name="final_norm",
    )(h, g)


def _rot_cols(w):
    half = w.shape[1] // 2
    return jnp.concatenate([-w[:, half:], w[:, :half]], axis=1)


def _gate_pairs(w_a, w_i):
    def bd(w):
        z = jnp.zeros((RNN_BLOCK, RNN_BLOCK), w.dtype)
        return jnp.stack([jnp.block([[w[2 * p], z], [z, w[2 * p + 1]]]) for p in range(N_RNN_BLOCKS // 2)])
    return jnp.concatenate([bd(w_a), bd(w_i)], axis=2).astype(BF16)


def _q_up_aug(w_uq):
    w = w_uq.reshape(Q_LORA, N_HEADS, QK_NOPE + QK_ROPE)
    pe = w[:, :, QK_NOPE:]
    rot = jnp.concatenate([-pe[:, :, QK_ROPE // 2:], pe[:, :, :QK_ROPE // 2]], axis=2)
    return jnp.concatenate([w, rot], axis=2).reshape(Q_LORA, N_HEADS * HEAD_PAD).astype(BF16)


def _kv_up_aug(w_up):
    w = w_up.reshape(KV_LORA, N_HEADS, QK_NOPE + V_HEAD)
    pad = HEAD_PAD - QK_NOPE
    w_k_top = jnp.concatenate([w[:, :, :QK_NOPE], jnp.zeros((KV_LORA, N_HEADS, pad), w.dtype)], axis=2)
    eye = jnp.concatenate([jnp.zeros((QK_ROPE, QK_NOPE), w.dtype), jnp.eye(QK_ROPE, dtype=w.dtype),
                           jnp.zeros((QK_ROPE, pad - QK_ROPE), w.dtype)], axis=1)
    w_k_bot = jnp.broadcast_to(eye[:, None, :], (QK_ROPE, N_HEADS, HEAD_PAD))
    w_k = jnp.concatenate([w_k_top, w_k_bot], axis=0).reshape(KV_LORA + QK_ROPE, N_HEADS * HEAD_PAD)
    w_v = w[:, :, QK_NOPE:].reshape(KV_LORA, N_HEADS * V_HEAD)
    return w_k.astype(BF16), w_v.astype(BF16)


def _rope_tables(t):
    inv_freq = ROPE_THETA ** (-jnp.arange(0, QK_ROPE, 2, dtype=F32) / QK_ROPE)
    ang = jnp.arange(t, dtype=F32)[:, None] * inv_freq[None, :]
    cos = jnp.tile(jnp.cos(ang), (1, 2))
    sin = jnp.tile(jnp.sin(ang), (1, 2))
    cs_k = jnp.concatenate([cos, sin], axis=1)
    ones = jnp.ones((t, QK_NOPE), F32)
    zeros = jnp.zeros((t, HEAD_PAD - QK_NOPE - QK_ROPE), F32)
    ctab = ATTN_SCALE * jnp.concatenate([ones, cos, zeros], axis=1)
    stab = ATTN_SCALE * jnp.concatenate([0.0 * ones, sin, zeros], axis=1)
    return cs_k, ctab, stab


def kernel(x, meta_tokens, norm_mix, norm_ffn, norm_final, rnn_w_in, rnn_conv_w, rnn_conv_b, rnn_w_a, rnn_b_a, rnn_w_i, rnn_b_i, rnn_lambda, rnn_w_out, kv_norm_src, kv_w_down, kv_latent_norm, kv_w_up, q_w_down, q_latent_norm, q_w_up, attn_w_out, ffn_w1, ffn_w3, ffn_w2, moe_router, moe_w1, moe_w3, moe_w2):
    bsz = x.shape[0]
    meta = jnp.broadcast_to(meta_tokens.astype(x.dtype)[None], (bsz, N_META, x.shape[-1]))
    h = jnp.concatenate([meta, x], axis=1)
    t = h.shape[1]
    assert t % TT == 0 and (t - N_META) % Q_BLOCK == 0
    row = lambda v: v.reshape(1, -1)
    cs_k, ctab, stab = _rope_tables(t)

    k = v = None
    for layer in range(DEPTH):
        if layer < N_A_LAYERS:
            a = layer
            h = _rnn_layer(h, row(norm_mix[layer]), rnn_w_in[a].astype(BF16), rnn_conv_w[a], row(rnn_conv_b[a]),
                           _gate_pairs(rnn_w_a[a], rnn_w_i[a]), row(rnn_b_a[a]), row(rnn_b_i[a]),
                           row(rnn_lambda[a]), rnn_w_out[a].astype(BF16))
        else:
            b = layer - N_A_LAYERS
            if b == 0:
                w_down_aug = jnp.concatenate([kv_w_down, _rot_cols(kv_w_down[:, KV_LORA:])], axis=1).astype(BF16)
                w_k, w_v = _kv_up_aug(kv_w_up)
                k, v = _kv_proj(h, row(kv_norm_src), w_down_aug, row(kv_latent_norm), cs_k, w_k, w_v)
            q = _q_proj(h, row(norm_mix[layer]), q_w_down[b].astype(BF16), row(q_latent_norm[b]),
                        _q_up_aug(q_w_up[b]), ctab, stab)
            h = _o_proj(h, _attention(q, k, v), attn_w_out[b].astype(BF16))
        if layer % 2 == 0:
            e = layer // 2
            h = _ffn_layer(h, row(norm_ffn[layer]), ffn_w1[e].astype(BF16), ffn_w3[e].astype(BF16),
                           ffn_w2[e].astype(BF16), tf=1408)
        else:
            m = layer // 2
            h = _moe_layer(h, row(norm_ffn[layer]), moe_router[m], moe_w1[m].astype(BF16),
                           moe_w3[m].astype(BF16), moe_w2[m].astype(BF16), tf=1792)
    return _final_norm(h, row(norm_final))
```

```python
import functools
import math

import jax
import jax.numpy as jnp
from jax import lax
from jax.experimental import pallas as pl
from jax.experimental.pallas import tpu as pltpu

D_MODEL = 1024
N_META = 16
CHUNK = 64
NORM_EPS = 1e-6
DEPTH = 4
N_A_LAYERS = DEPTH // 2

D_RNN = D_MODEL
N_RNN_BLOCKS = 8
RNN_BLOCK = D_RNN // N_RNN_BLOCKS
CONV_WIDTH = 4
LRU_C = 8.0

N_HEADS = 16
QK_NOPE = 64
QK_ROPE = 32
V_HEAD = 64
Q_LORA = 384
KV_LORA = 256
ROPE_THETA = 10000.0
ATTN_SCALE = 1.0 / math.sqrt(QK_NOPE + QK_ROPE)
MASK_VALUE = -1e30

N_EXPERTS = 8

LANES = 128
SUBLANES = 8
VMEM_LIMIT = 56 * 1024 * 1024

HEAD_PAD = 128
Q_BLOCK = 512
TT = 688
SCAN_GROUPS = 2
TM = 512
SEG_ALIGN = 16
SEG_SIZES = (512, 256, 128, 64, 32, 16)
TILE_ROWS = 2 * TM + N_EXPERTS * SEG_ALIGN
F32 = jnp.float32
BF16 = jnp.bfloat16


def _cparams(n_axes):
    return pltpu.CompilerParams(dimension_semantics=("arbitrary",) * n_axes,
                                vmem_limit_bytes=VMEM_LIMIT)


def _rms(x, g):
    return x * lax.rsqrt(jnp.mean(x * x, axis=-1, keepdims=True) + NORM_EPS) * g


def _dot(a, b):
    return jnp.dot(a, b, preferred_element_type=F32)


def _dot_nt(a, b):
    return lax.dot_general(a, b, (((1,), (1,)), ((), ())), preferred_element_type=F32)


def _rnn_kernel(h_ref, g_ref, win_ref, cw_ref, cb_ref, wg_ref, ba_ref, bi_ref, lam_ref, wout_ref,
                o_ref, xpad_ref, yg_ref, a_ref, u_ref, carry_ref):
    t = pl.program_id(1)
    tt = h_ref.shape[1]

    @pl.when(t == 0)
    def _():
        xpad_ref[0:SUBLANES, :] = jnp.zeros((SUBLANES, D_RNN), F32)
        carry_ref[...] = jnp.zeros_like(carry_ref)

    x = h_ref[0]
    hn = _rms(x, g_ref[...]).astype(BF16)
    yg_ref[...] = jax.nn.gelu(_dot(hn, win_ref[:, :D_RNN]), approximate=True)
    xpad_ref[SUBLANES:, :] = _dot(hn, win_ref[:, D_RNN:])

    xc = cb_ref[...] + cw_ref[CONV_WIDTH - 1:CONV_WIDTH, :] * xpad_ref[SUBLANES:, :]
    for j in range(CONV_WIDTH - 1):
        off = SUBLANES - (CONV_WIDTH - 1) + j
        xc = xc + cw_ref[j:j + 1, :] * xpad_ref[off:off + tt, :]
    xpad_ref[0:SUBLANES, :] = xpad_ref[tt:tt + SUBLANES, :]

    xcb = xc.astype(BF16)
    log_coef = -LRU_C * jax.nn.softplus(-lam_ref[...])
    pair = 2 * RNN_BLOCK
    for p in range(N_RNN_BLOCKS // 2):
        sl = slice(p * pair, (p + 1) * pair)
        gates = _dot(xcb[:, sl], wg_ref[p])
        gate_r = jax.nn.sigmoid(gates[:, :pair] + ba_ref[:, sl])
        gate_i = jax.nn.sigmoid(gates[:, pair:] + bi_ref[:, sl])
        log_a = log_coef[:, sl] * gate_r
        a_ref[:, sl] = jnp.exp(log_a)
        th = jnp.tanh(log_a)
        u_ref[:, sl] = jnp.sqrt(-2.0 * th / (1.0 - th)) * (gate_i * xc[:, sl])

    row = lax.broadcasted_iota(jnp.int32, (SUBLANES, D_RNN), 0)

    def groups(i, carry):
        local = []
        for g in range(SCAN_GROUPS):
            r0 = pl.multiple_of((i * SCAN_GROUPS + g) * SUBLANES, SUBLANES)
            a = a_ref[pl.ds(r0, SUBLANES), :]
            u = u_ref[pl.ds(r0, SUBLANES), :]
            for s in (1, 2, 4):
                keep = row >= s
                u = jnp.where(keep, a * pltpu.roll(u, s, axis=0) + u, u)
                a = jnp.where(keep, a * pltpu.roll(a, s, axis=0), a)
            local.append((r0, a, u))
        for r0, a, u in local:
            hs = a * carry + u
            u_ref[pl.ds(r0, SUBLANES), :] = hs
            carry = hs[SUBLANES - 1:SUBLANES, :]
        return carry

    carry_ref[...] = lax.fori_loop(0, tt // (SCAN_GROUPS * SUBLANES), groups, carry_ref[...])

    y = (u_ref[...] * yg_ref[...]).astype(BF16)
    o_ref[0] = x + _dot(y, wout_ref[...])


def _rnn_layer(h, g, w_in, conv_w, conv_b, w_gate, b_a, b_i, lam, w_out):
    bsz, t, d = h.shape
    nt = t // TT
    full = lambda shape: pl.BlockSpec(shape, lambda b, i: (0,) * len(shape))
    tile = pl.BlockSpec((1, TT, d), lambda b, i: (b, i, 0))
    return pl.pallas_call(
        _rnn_kernel,
        grid=(bsz, nt),
        in_specs=[tile, full((1, d)), full((d, 2 * D_RNN)), full((CONV_WIDTH, D_RNN)), full((1, D_RNN)),
                  full(w_gate.shape), full((1, D_RNN)), full((1, D_RNN)), full((1, D_RNN)),
                  full((D_RNN, d))],
        out_specs=tile,
        out_shape=jax.ShapeDtypeStruct(h.shape, F32),
        scratch_shapes=[pltpu.VMEM((TT + SUBLANES, D_RNN), F32), pltpu.VMEM((TT, D_RNN), F32),
                        pltpu.VMEM((TT, D_RNN), F32), pltpu.VMEM((TT, D_RNN), F32),
                        pltpu.VMEM((1, D_RNN), F32)],
        compiler_params=_cparams(2),
        name="rnn_mixer",
    )(h, g, w_in, conv_w, conv_b, w_gate, b_a, b_i, lam, w_out)


def _ffn_kernel(*refs, with_attn):
    if with_attn:
        h_ref, a_ref, wo_ref, g_ref, w1_ref, w3_ref, w2_ref, o_ref, hmid_ref, hf_ref, acc_ref = refs
    else:
        h_ref, g_ref, w1_ref, w3_ref, w2_ref, o_ref, hmid_ref, hf_ref, acc_ref = refs
    f = pl.program_id(2)

    @pl.when(f == 0)
    def _():
        hmid = h_ref[0]
        if with_attn:
            hmid = hmid + _dot(a_ref[0], wo_ref[...])
        hmid_ref[...] = hmid
        hf_ref[...] = _rms(hmid, g_ref[...]).astype(BF16)
        acc_ref[...] = jnp.zeros_like(acc_ref)

    hf = hf_ref[...]
    act = jax.nn.silu(_dot(hf, w1_ref[...])) * _dot(hf, w3_ref[...])
    acc_ref[...] += _dot(act.astype(BF16), w2_ref[...])

    @pl.when(f == pl.num_programs(2) - 1)
    def _():
        o_ref[0] = hmid_ref[...] + acc_ref[...]


def _ffn_layer(h, attn, w_o, g, w1, w3, w2, e, tf):
    bsz, t, d = h.shape
    d_ff = w1.shape[2]
    nt, nf = t // TT, d_ff // tf
    tile = pl.BlockSpec((1, TT, d), lambda b, i, f: (b, i, 0))
    attn_args, attn_specs = (), []
    if attn is not None:
        attn_args = (attn, w_o)
        attn_specs = [pl.BlockSpec((1, TT, attn.shape[2]), lambda b, i, f: (b, i, 0)),
                      pl.BlockSpec(w_o.shape, lambda b, i, f: (0, 0))]
    return pl.pallas_call(
        functools.partial(_ffn_kernel, with_attn=attn is not None),
        grid=(bsz, nt, nf),
        in_specs=[tile] + attn_specs +
                 [pl.BlockSpec((1, d), lambda b, i, f: (0, 0)),
                  pl.BlockSpec((None, d, tf), lambda b, i, f: (e, 0, f)),
                  pl.BlockSpec((None, d, tf), lambda b, i, f: (e, 0, f)),
                  pl.BlockSpec((None, tf, d), lambda b, i, f: (e, f, 0))],
        out_specs=tile,
        out_shape=jax.ShapeDtypeStruct(h.shape, F32),
        scratch_shapes=[pltpu.VMEM((TT, d), F32), pltpu.VMEM((TT, d), BF16), pltpu.VMEM((TT, d), F32)],
        compiler_params=_cparams(3),
        name="ffn_dense",
    )(h, *attn_args, g, w1, w3, w2)


def _router_kernel(h_ref, g_ref, wrt_ref, hf_ref, route_ref, cnt_ref):
    hf = _rms(h_ref[...], g_ref[...])
    hf_ref[...] = hf.astype(BF16)
    logits = lax.dot_general(wrt_ref[...], hf, (((1,), (1,)), ((), ())), preferred_element_type=F32,
                             precision=lax.Precision.HIGHEST)
    eidx = lax.broadcasted_iota(jnp.int32, logits.shape, 0)
    m1 = jnp.max(logits, axis=0, keepdims=True)
    i1 = jnp.min(jnp.where(logits == m1, eidx, N_EXPERTS), axis=0, keepdims=True)
    rest = jnp.where(eidx == i1, -jnp.inf, logits)
    m2 = jnp.max(rest, axis=0, keepdims=True)
    i2 = jnp.min(jnp.where(rest == m2, eidx, N_EXPERTS), axis=0, keepdims=True)
    ex = jnp.exp(m2 - m1)
    inv = 1.0 / (1.0 + ex)
    sel = jnp.where(eidx == i1, 1.0, 0.0) + jnp.where(eidx == i2, 1.0, 0.0)
    cnt_ref[0] = jnp.broadcast_to(jnp.sum(sel, axis=1, keepdims=True), cnt_ref.shape[1:])
    route_ref[0] = jnp.where(eidx == 0, i1.astype(F32), jnp.where(eidx == 1, i2.astype(F32),
                             jnp.where(eidx == 2, inv, jnp.where(eidx == 3, ex * inv, 0.0))))


def _router(h2d, g, w_router_t):
    m, d = h2d.shape
    nt = m // TM
    return pl.pallas_call(
        _router_kernel,
        grid=(nt,),
        in_specs=[pl.BlockSpec((TM, d), lambda t: (t, 0)), pl.BlockSpec((1, d), lambda t: (0, 0)),
                  pl.BlockSpec((N_EXPERTS, d), lambda t: (0, 0))],
        out_specs=[pl.BlockSpec((TM, d), lambda t: (t, 0)),
                   pl.BlockSpec((1, N_EXPERTS, TM), lambda t: (t, 0, 0)),
                   pl.BlockSpec((1, N_EXPERTS, LANES), lambda t: (t, 0, 0))],
        out_shape=[jax.ShapeDtypeStruct((m, d), BF16),
                   jax.ShapeDtypeStruct((nt, N_EXPERTS, TM), F32),
                   jax.ShapeDtypeStruct((nt, N_EXPERTS, LANES), F32)],
        compiler_params=_cparams(1),
        name="moe_router",
    )(h2d, g, w_router_t)


def _segment_copies(t, seg_ref, len_ref, base_ref, make_copy, start):
    for e in range(N_EXPERTS):
        n = len_ref[t * N_EXPERTS + e]
        local0 = seg_ref[t * N_EXPERTS + e]
        global0 = base_ref[t * N_EXPERTS + e]
        off = 0
        for p in SEG_SIZES:
            take = (n & p) != 0

            @pl.when(take)
            def _(off=off, p=p):
                cp = make_copy(pl.multiple_of(local0 + off, SEG_ALIGN), pl.multiple_of(global0 + off, SEG_ALIGN), p)
                if start:
                    cp.start()
                else:
                    cp.wait()

            off = off + jnp.where(take, p, 0)


def _dispatch_kernel(seg_ref, len_ref, base_ref, hf_ref, route_ref, tri_ref, xin_ref, xs_ref, slot_ref,
                     stage_ref, sem):
    del xin_ref
    t = pl.program_id(0)
    route = route_ref[0]
    eidx = lax.broadcasted_iota(jnp.int32, route.shape, 0)
    sel1 = eidx == route[0:1, :].astype(jnp.int32)
    sel2 = eidx == route[1:2, :].astype(jnp.int32)
    sel = jnp.where(sel1, 1.0, 0.0) + jnp.where(sel2, 1.0, 0.0)
    sel16 = jnp.concatenate([sel, jnp.zeros_like(sel)], axis=0).astype(BF16)
    rank = _dot(sel16, tri_ref[...])[:N_EXPERTS]
    seg = jnp.zeros(route.shape, jnp.int32)
    for e in range(N_EXPERTS):
        seg = jnp.where(eidx == e, seg_ref[t * N_EXPERTS + e], seg)
    slot = seg.astype(F32) + rank
    slot1 = jnp.sum(jnp.where(sel1, slot, 0.0), axis=0, keepdims=True)
    slot2 = jnp.sum(jnp.where(sel2, slot, 0.0), axis=0, keepdims=True)
    slot_ref[0] = jnp.where(eidx == 0, slot1, jnp.where(eidx == 1, slot2, 0.0))
    r = lax.broadcasted_iota(jnp.int32, (TILE_ROWS, TM), 0)
    perm = jnp.where(r == slot1.astype(jnp.int32), 1.0, jnp.where(r == slot2.astype(jnp.int32), 1.0, 0.0))
    stage_ref[...] = _dot(perm.astype(BF16), hf_ref[...]).astype(BF16)

    def copy(local, glob, p):
        return pltpu.make_async_copy(stage_ref.at[pl.ds(local, p)], xs_ref.at[pl.ds(glob, p)], sem)

    _segment_copies(t, seg_ref, len_ref, base_ref, copy, start=True)
    _segment_copies(t, seg_ref, len_ref, base_ref, copy, start=False)


def _dispatch(seg, seg_len, base, hf, route, tri, x_zero):
    m, d = hf.shape
    nt = m // TM
    grid_spec = pltpu.PrefetchScalarGridSpec(
        num_scalar_prefetch=3,
        grid=(nt,),
        in_specs=[pl.BlockSpec((TM, d), lambda t, *_: (t, 0)),
                  pl.BlockSpec((1, N_EXPERTS, TM), lambda t, *_: (t, 0, 0)),
                  pl.BlockSpec((TM, TM), lambda t, *_: (0, 0)),
                  pl.BlockSpec(memory_space=pl.ANY)],
        out_specs=[pl.BlockSpec(memory_space=pl.ANY),
                   pl.BlockSpec((1, N_EXPERTS, TM), lambda t, *_: (t, 0, 0))],
        scratch_shapes=[pltpu.VMEM((TILE_ROWS, d), BF16), pltpu.SemaphoreType.DMA(())],
    )
    return pl.pallas_call(
        _dispatch_kernel,
        grid_spec=grid_spec,
        out_shape=[jax.ShapeDtypeStruct(x_zero.shape, BF16),
                   jax.ShapeDtypeStruct((nt, N_EXPERTS, TM), F32)],
        input_output_aliases={6: 0},
        compiler_params=_cparams(1),
        name="moe_dispatch",
    )(seg, seg_len, base, hf, route, tri, x_zero)


def _experts_kernel(te_ref, na_ref, x_ref, w1_ref, w3_ref, w2_ref, o_ref, acc_ref):
    del te_ref
    i = pl.program_id(0)
    f = pl.program_id(1)

    @pl.when(i < na_ref[0])
    def _():
        @pl.when(f == 0)
        def _():
            acc_ref[...] = jnp.zeros_like(acc_ref)

        x = x_ref[...]
        act = jax.nn.silu(_dot(x, w1_ref[0])) * _dot(x, w3_ref[0])
        acc_ref[...] += _dot(act.astype(BF16), w2_ref[0])

        @pl.when(f == pl.num_programs(1) - 1)
        def _():
            o_ref[...] = acc_ref[...].astype(BF16)

    @pl.when(jnp.logical_and(i >= na_ref[0], f == pl.num_programs(1) - 1))
    def _():
        o_ref[...] = jnp.zeros_like(o_ref)


def _experts(tile_expert, n_active, x_sorted, w1, w3, w2, layer, tf):
    rows, d = x_sorted.shape
    d_ff = w1.shape[3]
    n_tiles, nf = rows // TM, d_ff // tf
    row_blk = lambda i, f, te, na: (jnp.minimum(i, na[0] - 1), 0)
    chunk = lambda i, f, na: jnp.where(i < na[0], f, nf - 1)
    grid_spec = pltpu.PrefetchScalarGridSpec(
        num_scalar_prefetch=2,
        grid=(n_tiles, nf),
        in_specs=[pl.BlockSpec((TM, d), row_blk),
                  pl.BlockSpec((None, 1, d, tf), lambda i, f, te, na: (layer, te[i], 0, chunk(i, f, na))),
                  pl.BlockSpec((None, 1, d, tf), lambda i, f, te, na: (layer, te[i], 0, chunk(i, f, na))),
                  pl.BlockSpec((None, 1, tf, d), lambda i, f, te, na: (layer, te[i], chunk(i, f, na), 0))],
        out_specs=pl.BlockSpec((TM, d), lambda i, f, te, na: (i, 0)),
        scratch_shapes=[pltpu.VMEM((TM, d), F32)],
    )
    return pl.pallas_call(
        _experts_kernel,
        grid_spec=grid_spec,
        out_shape=jax.ShapeDtypeStruct((rows, d), BF16),
        compiler_params=_cparams(2),
        name="moe_experts",
    )(tile_expert, n_active, x_sorted, w1, w3, w2)


def _combine_kernel(seg_ref, len_ref, base_ref, h_ref, slot_ref, gate_ref, y_ref, o_ref, ybuf_ref, sem):
    t = pl.program_id(0)
    buf = t % 2

    def fetch(tile, b, start):
        def copy(local, glob, p):
            return pltpu.make_async_copy(y_ref.at[pl.ds(glob, p)], ybuf_ref.at[b, pl.ds(local, p)], sem.at[b])

        if start:
            ybuf_ref[b, 2 * TM:, :] = jnp.zeros((TILE_ROWS - 2 * TM, ybuf_ref.shape[2]), BF16)
        _segment_copies(tile, seg_ref, len_ref, base_ref, copy, start=start)

    @pl.when(t == 0)
    def _():
        fetch(t, buf, start=True)

    @pl.when(t + 1 < pl.num_programs(0))
    def _():
        fetch(t + 1, 1 - buf, start=True)

    fetch(t, buf, start=False)

    y = ybuf_ref[buf]
    lane = lax.broadcasted_iota(jnp.int32, (TM, TILE_ROWS), 1)
    out = h_ref[...]
    for k in range(2):
        onehot = jnp.where(lane == slot_ref[:, k:k + 1], 1.0, 0.0).astype(BF16)
        out = out + gate_ref[:, k:k + 1] * _dot(onehot, y)
    o_ref[...] = out


def _combine(seg, seg_len, base, h2d, slots, gates, y_sorted):
    m, d = h2d.shape
    nt = m // TM
    grid_spec = pltpu.PrefetchScalarGridSpec(
        num_scalar_prefetch=3,
        grid=(nt,),
        in_specs=[pl.BlockSpec((TM, d), lambda t, *_: (t, 0)),
                  pl.BlockSpec((TM, 2), lambda t, *_: (t, 0)),
                  pl.BlockSpec((TM, 2), lambda t, *_: (t, 0)),
                  pl.BlockSpec(memory_space=pl.ANY)],
        out_specs=pl.BlockSpec((TM, d), lambda t, *_: (t, 0)),
        scratch_shapes=[pltpu.VMEM((2, TILE_ROWS, d), BF16), pltpu.SemaphoreType.DMA((2,))],
    )
    return pl.pallas_call(
        _combine_kernel,
        grid_spec=grid_spec,
        out_shape=jax.ShapeDtypeStruct((m, d), F32),
        compiler_params=_cparams(1),
        name="moe_combine",
    )(seg, seg_len, base, h2d, slots, gates, y_sorted)


def _moe_tables(counts, n_tiles_max):
    pad = (counts + SEG_ALIGN - 1) // SEG_ALIGN * SEG_ALIGN
    seg = jnp.cumsum(pad, axis=1) - pad
    region = (jnp.sum(pad, axis=0) + TM - 1) // TM * TM
    base = (jnp.cumsum(region) - region)[None, :] + jnp.cumsum(pad, axis=0) - pad
    tiles_end = jnp.cumsum(region // TM)
    n_active = tiles_end[-1]
    tile = jnp.minimum(jnp.arange(n_tiles_max), n_active - 1)
    tile_expert = jnp.sum(tile[:, None] >= tiles_end[None, :], axis=1)
    flat = lambda a: a.reshape(-1).astype(jnp.int32)
    return flat(seg), flat(pad), flat(base), flat(tile_expert), flat(n_active)


def _moe_layer(h, g, w_router, w1, w3, w2, layer, tf):
    bsz, t, d = h.shape
    m = bsz * t
    nt = m // TM
    n_tiles_max = -(-(nt * (TILE_ROWS - SEG_ALIGN) + N_EXPERTS * (TM - 1)) // TM)
    h2d = h.reshape(m, d)
    hf, route, counts = _router(h2d, g, w_router.T)
    seg, seg_len, base, tile_expert, n_active = _moe_tables(counts[:, :, 0].astype(jnp.int32), n_tiles_max)
    tri = (jnp.arange(TM)[:, None] < jnp.arange(TM)[None, :]).astype(BF16)
    x_sorted, slots = _dispatch(seg, seg_len, base, hf, route, tri, jnp.zeros((n_tiles_max * TM, d), BF16))
    y_sorted = _experts(tile_expert, n_active, x_sorted, w1, w3, w2, layer, tf)
    to_cols = lambda a: a.transpose(0, 2, 1).reshape(m, 2)
    out = _combine(seg, seg_len, base, h2d, to_cols(slots[:, :2, :]).astype(jnp.int32),
                   to_cols(route[:, 2:4, :]), y_sorted)
    return out.reshape(bsz, t, d)


def _kv_kernel(h_ref, g_ref, wd_ref, gl_ref, cs_ref, wk_ref, wv_ref, k_ref, v_ref):
    hs = _rms(h_ref[0], g_ref[...]).astype(BF16)
    ckv = _dot(hs, wd_ref[...])
    c_lat = _rms(ckv[:, :KV_LORA], gl_ref[...])
    pe = ckv[:, KV_LORA:KV_LORA + QK_ROPE]
    pe_rot = ckv[:, KV_LORA + QK_ROPE:]
    k_pe = pe * cs_ref[:, :QK_ROPE] + pe_rot * cs_ref[:, QK_ROPE:]
    lat = jnp.concatenate([c_lat, k_pe], axis=1).astype(BF16)
    k_ref[0] = _dot(lat, wk_ref[...]).astype(BF16)
    v_ref[0] = _dot(lat[:, :KV_LORA], wv_ref[...]).astype(BF16)


def _kv_proj(h, g_src, w_down_aug, g_latent, cs_k, w_k, w_v):
    bsz, t, d = h.shape
    nt = t // TT
    full = lambda shape: pl.BlockSpec(shape, lambda b, i: (0,) * len(shape))
    return pl.pallas_call(
        _kv_kernel,
        grid=(bsz, nt),
        in_specs=[pl.BlockSpec((1, TT, d), lambda b, i: (b, i, 0)), full((1, d)), full(w_down_aug.shape),
                  full((1, KV_LORA)), pl.BlockSpec((TT, 2 * QK_ROPE), lambda b, i: (i, 0)),
                  full(w_k.shape), full(w_v.shape)],
        out_specs=[pl.BlockSpec((1, TT, N_HEADS * HEAD_PAD), lambda b, i: (b, i, 0)),
                   pl.BlockSpec((1, TT, N_HEADS * V_HEAD), lambda b, i: (b, i, 0))],
        out_shape=[jax.ShapeDtypeStruct((bsz, t, N_HEADS * HEAD_PAD), BF16),
                   jax.ShapeDtypeStruct((bsz, t, N_HEADS * V_HEAD), BF16)],
        compiler_params=_cparams(2),
        name="kv_proj",
    )(h, g_src, w_down_aug, g_latent, cs_k, w_k, w_v)


def _q_kernel(h_ref, g_ref, wdq_ref, gq_ref, wuq_ref, ct_ref, st_ref, q_ref):
    hn = _rms(h_ref[0], g_ref[...]).astype(BF16)
    c_q = _rms(_dot(hn, wdq_ref[...]), gq_ref[...]).astype(BF16)
    q = _dot(c_q, wuq_ref[...])
    width = q.shape[1]
    ct = jnp.tile(ct_ref[...], (1, N_HEADS))
    st = jnp.tile(st_ref[...], (1, N_HEADS))
    q_ref[0] = (q * ct + pltpu.roll(q, width - QK_ROPE, axis=1) * st).astype(BF16)


def _q_proj(h, g, w_dq, g_q, w_uq_aug, ctab, stab):
    bsz, t, d = h.shape
    nt = t // TT
    full = lambda shape: pl.BlockSpec(shape, lambda b, i: (0,) * len(shape))
    tab = pl.BlockSpec((TT, HEAD_PAD), lambda b, i: (i, 0))
    return pl.pallas_call(
        _q_kernel,
        grid=(bsz, nt),
        in_specs=[pl.BlockSpec((1, TT, d), lambda b, i: (b, i, 0)), full((1, d)), full(w_dq.shape),
                  full((1, Q_LORA)), full(w_uq_aug.shape), tab, tab],
        out_specs=pl.BlockSpec((1, TT, N_HEADS * HEAD_PAD), lambda b, i: (b, i, 0)),
        out_shape=jax.ShapeDtypeStruct((bsz, t, N_HEADS * HEAD_PAD), BF16),
        compiler_params=_cparams(2),
        name="q_proj",
    )(h, g, w_dq, g_q, w_uq_aug, ctab, stab)


def _attn_kernel(q_ref, k_ref, v_ref, o_ref, s_ref):
    n_qblk = (q_ref.shape[1] - N_META) // Q_BLOCK
    heads = (0, 1)
    qk_sl = [slice(hh * HEAD_PAD, (hh + 1) * HEAD_PAD) for hh in heads]
    pair_v = 2 * V_HEAD

    def v_ext(vv):
        n = vv.shape[0]
        head0 = lax.broadcasted_iota(jnp.int32, (n, pair_v), 1) < V_HEAD
        sel0 = jnp.where(head0, 1.0, 0.0).astype(BF16)
        sel1 = jnp.where(head0, 0.0, 1.0).astype(BF16)
        r = lax.broadcasted_iota(jnp.int32, (2 * n, pair_v), 0)
        c = lax.broadcasted_iota(jnp.int32, (2 * n, pair_v), 1)
        ones = jnp.where(c == jnp.where(r < n, 0, 1), 1.0, 0.0).astype(BF16)
        return jnp.concatenate([jnp.concatenate([vv * sel0, vv * sel1], axis=0), ones], axis=1)

    def normalise(acc):
        head0 = lax.broadcasted_iota(jnp.int32, (acc.shape[0], pair_v), 1) < V_HEAD
        inv = jnp.where(head0, 1.0 / acc[:, pair_v:pair_v + 1], 1.0 / acc[:, pair_v + 1:pair_v + 2])
        return (acc[:, :pair_v] * inv).astype(BF16)

    k_meta = [k_ref[0, 0:N_META, qk_sl[hh]] for hh in heads]
    v_meta = v_ext(v_ref[0, 0:N_META, :])

    ps = []
    for hh in heads:
        s = _dot_nt(q_ref[0, 0:N_META, qk_sl[hh]], k_meta[hh])
        ps.append(jnp.exp(s - jnp.max(s, axis=1, keepdims=True)))
    o_ref[0, 0:N_META, :] = normalise(_dot(jnp.concatenate(ps, axis=1).astype(BF16), v_meta))

    rc = lax.broadcasted_iota(jnp.int32, (Q_BLOCK, Q_BLOCK), 0) // CHUNK
    cc = lax.broadcasted_iota(jnp.int32, (Q_BLOCK, Q_BLOCK), 1) // CHUNK
    diag_mask = cc <= rc

    def fold(s):
        tiles = [s[:, c:c + LANES] for c in range(0, s.shape[1], LANES)]
        return functools.reduce(jnp.maximum, tiles)

    for i in range(n_qblk):
        r0 = N_META + i * Q_BLOCK
        qs = [q_ref[0, r0:r0 + Q_BLOCK, qk_sl[hh]] for hh in heads]
        s_meta = [_dot_nt(qs[hh], k_meta[hh]) for hh in heads]
        mf = [None, None]
        for j in range(i + 1):
            c0 = N_META + j * Q_BLOCK
            for hh in heads:
                s = _dot_nt(qs[hh], k_ref[0, c0:c0 + Q_BLOCK, qk_sl[hh]])
                if j == i:
                    s = jnp.where(diag_mask, s, MASK_VALUE)
                s_ref[j, hh] = s
                mf[hh] = fold(s) if mf[hh] is None else jnp.maximum(mf[hh], fold(s))
        mb = []
        for hh in heads:
            m = jnp.maximum(jnp.max(mf[hh], axis=1, keepdims=True), jnp.max(s_meta[hh], axis=1, keepdims=True))
            mb.append(jnp.broadcast_to(m, (Q_BLOCK, LANES)))
        p_meta = [jnp.exp(s_meta[hh] - mb[hh][:, :N_META]).astype(BF16) for hh in heads]
        acc = _dot(jnp.concatenate(p_meta, axis=1), v_meta)
        for j in range(i + 1):
            c0 = N_META + j * Q_BLOCK
            p = [jnp.exp(s_ref[j, hh] - jnp.concatenate([mb[hh]] * (Q_BLOCK // LANES), axis=1)).astype(BF16)
                 for hh in heads]
            acc = acc + _dot(jnp.concatenate(p, axis=1), v_ext(v_ref[0, c0:c0 + Q_BLOCK, :]))
        o_ref[0, r0:r0 + Q_BLOCK, :] = normalise(acc)


def _attention(q, k, v):
    bsz, t, _ = q.shape
    qk_spec = pl.BlockSpec((1, t, 2 * HEAD_PAD), lambda b, p: (b, 0, p))
    v_spec = pl.BlockSpec((1, t, 2 * V_HEAD), lambda b, p: (b, 0, p))
    return pl.pallas_call(
        _attn_kernel,
        grid=(bsz, N_HEADS // 2),
        in_specs=[qk_spec, qk_spec, v_spec],
        out_specs=v_spec,
        out_shape=jax.ShapeDtypeStruct((bsz, t, N_HEADS * V_HEAD), BF16),
        scratch_shapes=[pltpu.VMEM(((t - N_META) // Q_BLOCK, 2, Q_BLOCK, Q_BLOCK), F32)],
        compiler_params=_cparams(2),
        name="attention",
    )(q, k, v)


def _oproj_kernel(h_ref, a_ref, w_ref, o_ref):
    o_ref[0] = h_ref[0] + _dot(a_ref[0], w_ref[...])


def _o_proj(h, attn, w_o):
    bsz, t, d = h.shape
    nt = t // TT
    tile = pl.BlockSpec((1, TT, d), lambda b, i: (b, i, 0))
    return pl.pallas_call(
        _oproj_kernel,
        grid=(bsz, nt),
        in_specs=[tile, pl.BlockSpec((1, TT, attn.shape[2]), lambda b, i: (b, i, 0)),
                  pl.BlockSpec(w_o.shape, lambda b, i: (0, 0))],
        out_specs=tile,
        out_shape=jax.ShapeDtypeStruct(h.shape, F32),
        compiler_params=_cparams(2),
        name="o_proj",
    )(h, attn, w_o)


def _final_kernel(h_ref, g_ref, o_ref):
    o_ref[0] = _rms(h_ref[0, N_META:, :], g_ref[...])


def _final_norm(h, g):
    bsz, t, d = h.shape
    return pl.pallas_call(
        _final_kernel,
        grid=(bsz,),
        in_specs=[pl.BlockSpec((1, t, d), lambda b: (b, 0, 0)), pl.BlockSpec((1, d), lambda b: (0, 0))],
        out_specs=pl.BlockSpec((1, t - N_META, d), lambda b: (b, 0, 0)),
        out_shape=jax.ShapeDtypeStruct((bsz, t - N_META, d), F32),
        compiler_params=_cparams(1),
        name="final_norm",
    )(h, g)


def _rot_cols(w):
    half = w.shape[1] // 2
    return jnp.concatenate([-w[:, half:], w[:, :half]], axis=1)


def _gate_pairs(w_a, w_i):
    def bd(w):
        z = jnp.zeros((RNN_BLOCK, RNN_BLOCK), w.dtype)
        return jnp.stack([jnp.block([[w[2 * p], z], [z, w[2 * p + 1]]]) for p in range(N_RNN_BLOCKS // 2)])
    return jnp.concatenate([bd(w_a), bd(w_i)], axis=2).astype(BF16)


def _q_up_aug(w_uq):
    w = w_uq.reshape(Q_LORA, N_HEADS, QK_NOPE + QK_ROPE)
    pe = w[:, :, QK_NOPE:]
    rot = jnp.concatenate([-pe[:, :, QK_ROPE // 2:], pe[:, :, :QK_ROPE // 2]], axis=2)
    return jnp.concatenate([w, rot], axis=2).reshape(Q_LORA, N_HEADS * HEAD_PAD).astype(BF16)


def _kv_up_aug(w_up):
    w = w_up.reshape(KV_LORA, N_HEADS, QK_NOPE + V_HEAD)
    pad = HEAD_PAD - QK_NOPE
    w_k_top = jnp.concatenate([w[:, :, :QK_NOPE], jnp.zeros((KV_LORA, N_HEADS, pad), w.dtype)], axis=2)
    eye = jnp.concatenate([jnp.zeros((QK_ROPE, QK_NOPE), w.dtype), jnp.eye(QK_ROPE, dtype=w.dtype),
                           jnp.zeros((QK_ROPE, pad - QK_ROPE), w.dtype)], axis=1)
    w_k_bot = jnp.broadcast_to(eye[:, None, :], (QK_ROPE, N_HEADS, HEAD_PAD))
    w_k = jnp.concatenate([w_k_top, w_k_bot], axis=0).reshape(KV_LORA + QK_ROPE, N_HEADS * HEAD_PAD)
    w_v = w[:, :, QK_NOPE:].reshape(KV_LORA, N_HEADS * V_HEAD)
    return w_k.astype(BF16), w_v.astype(BF16)


def _rope_tables(t):
    inv_freq = ROPE_THETA ** (-jnp.arange(0, QK_ROPE, 2, dtype=F32) / QK_ROPE)
    ang = jnp.arange(t, dtype=F32)[:, None] * inv_freq[None, :]
    cos = jnp.tile(jnp.cos(ang), (1, 2))
    sin = jnp.tile(jnp.sin(ang), (1, 2))
    cs_k = jnp.concatenate([cos, sin], axis=1)
    ones = jnp.ones((t, QK_NOPE), F32)
    zeros = jnp.zeros((t, HEAD_PAD - QK_NOPE - QK_ROPE), F32)
    ctab = ATTN_SCALE * jnp.concatenate([ones, cos, zeros], axis=1)
    stab = ATTN_SCALE * jnp.concatenate([0.0 * ones, sin, zeros], axis=1)
    return cs_k, ctab, stab


def kernel(x, meta_tokens, norm_mix, norm_ffn, norm_final, rnn_w_in, rnn_conv_w, rnn_conv_b, rnn_w_a, rnn_b_a, rnn_w_i, rnn_b_i, rnn_lambda, rnn_w_out, kv_norm_src, kv_w_down, kv_latent_norm, kv_w_up, q_w_down, q_latent_norm, q_w_up, attn_w_out, ffn_w1, ffn_w3, ffn_w2, moe_router, moe_w1, moe_w3, moe_w2):
    bsz = x.shape[0]
    meta = jnp.broadcast_to(meta_tokens.astype(x.dtype)[None], (bsz, N_META, x.shape[-1]))
    h = jnp.concatenate([meta, x], axis=1)
    t = h.shape[1]
    assert t % TT == 0 and (t - N_META) % Q_BLOCK == 0
    row = lambda v: v.reshape(1, -1)
    cs_k, ctab, stab = _rope_tables(t)

    ffn_w = [w.astype(BF16) for w in (ffn_w1, ffn_w3, ffn_w2)]
    moe_w = [w.astype(BF16) for w in (moe_w1, moe_w3, moe_w2)]
    k = v = attn = w_o = None
    for layer in range(DEPTH):
        if layer < N_A_LAYERS:
            a = layer
            h = _rnn_layer(h, row(norm_mix[layer]), rnn_w_in[a].astype(BF16), rnn_conv_w[a], row(rnn_conv_b[a]),
                           _gate_pairs(rnn_w_a[a], rnn_w_i[a]), row(rnn_b_a[a]), row(rnn_b_i[a]),
                           row(rnn_lambda[a]), rnn_w_out[a].astype(BF16))
        else:
            b = layer - N_A_LAYERS
            if b == 0:
                w_down_aug = jnp.concatenate([kv_w_down, _rot_cols(kv_w_down[:, KV_LORA:])], axis=1).astype(BF16)
                w_k, w_v = _kv_up_aug(kv_w_up)
                k, v = _kv_proj(h, row(kv_norm_src), w_down_aug, row(kv_latent_norm), cs_k, w_k, w_v)
            q = _q_proj(h, row(norm_mix[layer]), q_w_down[b].astype(BF16), row(q_latent_norm[b]),
                        _q_up_aug(q_w_up[b]), ctab, stab)
            attn, w_o = _attention(q, k, v), attn_w_out[b].astype(BF16)
        if layer % 2 == 0:
            h = _ffn_layer(h, attn, w_o, row(norm_ffn[layer]), *ffn_w, e=layer // 2, tf=1408)
        else:
            if attn is not None:
                h = _o_proj(h, attn, w_o)
            h = _moe_layer(h, row(norm_ffn[layer]), moe_router[layer // 2], *moe_w, layer=layer // 2, tf=1792)
        attn = None
    return _final_norm(h, row(norm_final))
```

```python
import functools
import math

import jax
import jax.numpy as jnp
from jax import lax
from jax.experimental import pallas as pl
from jax.experimental.pallas import tpu as pltpu

D_MODEL = 1024
N_META = 16
CHUNK = 64
NORM_EPS = 1e-6
DEPTH = 4
N_A_LAYERS = DEPTH // 2

D_RNN = D_MODEL
N_RNN_BLOCKS = 8
RNN_BLOCK = D_RNN // N_RNN_BLOCKS
CONV_WIDTH = 4
LRU_C = 8.0

N_HEADS = 16
QK_NOPE = 64
QK_ROPE = 32
V_HEAD = 64
Q_LORA = 384
KV_LORA = 256
ROPE_THETA = 10000.0
ATTN_SCALE = 1.0 / math.sqrt(QK_NOPE + QK_ROPE)
MASK_VALUE = -1e30

N_EXPERTS = 8

LANES = 128
SUBLANES = 8
VMEM_LIMIT = 56 * 1024 * 1024

HEAD_PAD = 128
Q_BLOCK = 256
TT = 688
FF_CHUNK = 1024
SCAN_GROUPS = 2
TM = 512
SEG_ALIGN = 16
SEG_SIZES = (512, 256, 128, 64, 32, 16)
TILE_ROWS = 2 * TM + N_EXPERTS * SEG_ALIGN
F32 = jnp.float32
BF16 = jnp.bfloat16


def _cparams(n_axes):
    return pltpu.CompilerParams(dimension_semantics=("arbitrary",) * n_axes,
                                vmem_limit_bytes=VMEM_LIMIT)


def _rms(x, g):
    return x * lax.rsqrt(jnp.mean(x * x, axis=-1, keepdims=True) + NORM_EPS) * g


def _dot(a, b):
    return jnp.dot(a, b, preferred_element_type=F32)


def _dot_nt(a, b):
    return lax.dot_general(a, b, (((1,), (1,)), ((), ())), preferred_element_type=F32)


def _rnn_kernel(h_ref, g_ref, win_ref, cw_ref, cb_ref, wg_ref, ba_ref, bi_ref, lam_ref, wout_ref,
                o_ref, xpad_ref, yg_ref, a_ref, u_ref, carry_ref):
    t = pl.program_id(1)
    tt = h_ref.shape[1]

    @pl.when(t == 0)
    def _():
        xpad_ref[0:SUBLANES, :] = jnp.zeros((SUBLANES, D_RNN), F32)
        carry_ref[...] = jnp.zeros_like(carry_ref)

    x = h_ref[0]
    hn = _rms(x, g_ref[...]).astype(BF16)
    yg_ref[...] = jax.nn.gelu(_dot(hn, win_ref[:, :D_RNN]), approximate=True)
    xpad_ref[SUBLANES:, :] = _dot(hn, win_ref[:, D_RNN:])

    xc = cb_ref[...] + cw_ref[CONV_WIDTH - 1:CONV_WIDTH, :] * xpad_ref[SUBLANES:, :]
    for j in range(CONV_WIDTH - 1):
        off = SUBLANES - (CONV_WIDTH - 1) + j
        xc = xc + cw_ref[j:j + 1, :] * xpad_ref[off:off + tt, :]
    xpad_ref[0:SUBLANES, :] = xpad_ref[tt:tt + SUBLANES, :]

    xcb = xc.astype(BF16)
    log_coef = -LRU_C * jax.nn.softplus(-lam_ref[...])
    pair = 2 * RNN_BLOCK
    for p in range(N_RNN_BLOCKS // 2):
        sl = slice(p * pair, (p + 1) * pair)
        gates = _dot(xcb[:, sl], wg_ref[p])
        gate_r = jax.nn.sigmoid(gates[:, :pair] + ba_ref[:, sl])
        gate_i = jax.nn.sigmoid(gates[:, pair:] + bi_ref[:, sl])
        log_a = log_coef[:, sl] * gate_r
        a_ref[:, sl] = jnp.exp(log_a)
        th = jnp.tanh(log_a)
        u_ref[:, sl] = jnp.sqrt(-2.0 * th / (1.0 - th)) * (gate_i * xc[:, sl])

    row = lax.broadcasted_iota(jnp.int32, (SUBLANES, D_RNN), 0)

    def groups(i, carry):
        local = []
        for g in range(SCAN_GROUPS):
            r0 = pl.multiple_of((i * SCAN_GROUPS + g) * SUBLANES, SUBLANES)
            a = a_ref[pl.ds(r0, SUBLANES), :]
            u = u_ref[pl.ds(r0, SUBLANES), :]
            for s in (1, 2, 4):
                keep = row >= s
                u = jnp.where(keep, a * pltpu.roll(u, s, axis=0) + u, u)
                a = jnp.where(keep, a * pltpu.roll(a, s, axis=0), a)
            local.append((r0, a, u))
        for r0, a, u in local:
            hs = a * carry + u
            u_ref[pl.ds(r0, SUBLANES), :] = hs
            carry = hs[SUBLANES - 1:SUBLANES, :]
        return carry

    carry_ref[...] = lax.fori_loop(0, tt // (SCAN_GROUPS * SUBLANES), groups, carry_ref[...])

    y = (u_ref[...] * yg_ref[...]).astype(BF16)
    o_ref[0] = x + _dot(y, wout_ref[...])


def _rnn_layer(h, g, w_in, conv_w, conv_b, w_gate, b_a, b_i, lam, w_out):
    bsz, t, d = h.shape
    nt = t // TT
    full = lambda shape: pl.BlockSpec(shape, lambda b, i: (0,) * len(shape))
    tile = pl.BlockSpec((1, TT, d), lambda b, i: (b, i, 0))
    return pl.pallas_call(
        _rnn_kernel,
        grid=(bsz, nt),
        in_specs=[tile, full((1, d)), full((d, 2 * D_RNN)), full((CONV_WIDTH, D_RNN)), full((1, D_RNN)),
                  full(w_gate.shape), full((1, D_RNN)), full((1, D_RNN)), full((1, D_RNN)),
                  full((D_RNN, d))],
        out_specs=tile,
        out_shape=jax.ShapeDtypeStruct(h.shape, F32),
        scratch_shapes=[pltpu.VMEM((TT + SUBLANES, D_RNN), F32), pltpu.VMEM((TT, D_RNN), F32),
                        pltpu.VMEM((TT, D_RNN), F32), pltpu.VMEM((TT, D_RNN), F32),
                        pltpu.VMEM((1, D_RNN), F32)],
        compiler_params=_cparams(2),
        name="rnn_mixer",
    )(h, g, w_in, conv_w, conv_b, w_gate, b_a, b_i, lam, w_out)


def _ffn_kernel(*refs, with_attn):
    if with_attn:
        h_ref, a_ref, wo_ref, g_ref, w1_ref, w3_ref, w2_ref, o_ref = refs
    else:
        h_ref, g_ref, w1_ref, w3_ref, w2_ref, o_ref = refs
    out = h_ref[0]
    if with_attn:
        out = out + _dot(a_ref[0], wo_ref[...])
    hf = _rms(out, g_ref[...]).astype(BF16)
    d_ff = w1_ref.shape[1]
    for c0 in range(0, d_ff, FF_CHUNK):
        c1 = min(c0 + FF_CHUNK, d_ff)
        act = jax.nn.silu(_dot(hf, w1_ref[:, c0:c1])) * _dot(hf, w3_ref[:, c0:c1])
        out = out + _dot(act.astype(BF16), w2_ref[c0:c1, :])
    o_ref[0] = out


def _ffn_layer(h, attn, w_o, g, w1, w3, w2, e):
    bsz, t, d = h.shape
    d_ff = w1.shape[2]
    nt = t // TT
    tile = pl.BlockSpec((1, TT, d), lambda b, i: (b, i, 0))
    resident = lambda shape, idx: pl.BlockSpec(shape, lambda b, i: idx, pipeline_mode=pl.Buffered(1))
    attn_args, attn_specs = (), []
    if attn is not None:
        attn_args = (attn, w_o)
        attn_specs = [pl.BlockSpec((1, TT, attn.shape[2]), lambda b, i: (b, i, 0)),
                      resident(w_o.shape, (0, 0))]
    return pl.pallas_call(
        functools.partial(_ffn_kernel, with_attn=attn is not None),
        grid=(bsz, nt),
        in_specs=[tile] + attn_specs +
                 [pl.BlockSpec((1, d), lambda b, i: (0, 0)),
                  resident((None, d, d_ff), (e, 0, 0)),
                  resident((None, d, d_ff), (e, 0, 0)),
                  resident((None, d_ff, d), (e, 0, 0))],
        out_specs=tile,
        out_shape=jax.ShapeDtypeStruct(h.shape, F32),
        compiler_params=_cparams(2),
        name="ffn_dense",
    )(h, *attn_args, g, w1, w3, w2)


def _router_kernel(*refs, with_attn):
    if with_attn:
        h_ref, a_ref, wo_ref, g_ref, wrt_ref, hmid_ref, hf_ref, route_ref, cnt_ref = refs
        hmid = h_ref[...] + _dot(a_ref[...], wo_ref[...])
        hmid_ref[...] = hmid
    else:
        h_ref, g_ref, wrt_ref, hf_ref, route_ref, cnt_ref = refs
        hmid = h_ref[...]
    hf = _rms(hmid, g_ref[...])
    hf_ref[...] = hf.astype(BF16)
    logits = lax.dot_general(wrt_ref[...], hf, (((1,), (1,)), ((), ())), preferred_element_type=F32,
                             precision=lax.Precision.HIGHEST)
    eidx = lax.broadcasted_iota(jnp.int32, logits.shape, 0)
    m1 = jnp.max(logits, axis=0, keepdims=True)
    i1 = jnp.min(jnp.where(logits == m1, eidx, N_EXPERTS), axis=0, keepdims=True)
    rest = jnp.where(eidx == i1, -jnp.inf, logits)
    m2 = jnp.max(rest, axis=0, keepdims=True)
    i2 = jnp.min(jnp.where(rest == m2, eidx, N_EXPERTS), axis=0, keepdims=True)
    ex = jnp.exp(m2 - m1)
    inv = 1.0 / (1.0 + ex)
    sel = jnp.where(eidx == i1, 1.0, 0.0) + jnp.where(eidx == i2, 1.0, 0.0)
    cnt_ref[0] = jnp.broadcast_to(jnp.sum(sel, axis=1, keepdims=True), cnt_ref.shape[1:])
    route_ref[0] = jnp.where(eidx == 0, i1.astype(F32), jnp.where(eidx == 1, i2.astype(F32),
                             jnp.where(eidx == 2, inv, jnp.where(eidx == 3, ex * inv, 0.0))))


def _router(h2d, attn2d, w_o, g, w_router_t):
    m, d = h2d.shape
    nt = m // TM
    rows = pl.BlockSpec((TM, d), lambda t: (t, 0))
    attn_args, attn_specs, hmid_specs, hmid_shapes = (), [], [], []
    if attn2d is not None:
        attn_args = (attn2d, w_o)
        attn_specs = [pl.BlockSpec((TM, attn2d.shape[1]), lambda t: (t, 0)), pl.BlockSpec(w_o.shape, lambda t: (0, 0))]
        hmid_specs, hmid_shapes = [rows], [jax.ShapeDtypeStruct((m, d), F32)]
    outs = pl.pallas_call(
        functools.partial(_router_kernel, with_attn=attn2d is not None),
        grid=(nt,),
        in_specs=[rows] + attn_specs + [pl.BlockSpec((1, d), lambda t: (0, 0)),
                                        pl.BlockSpec((N_EXPERTS, d), lambda t: (0, 0))],
        out_specs=hmid_specs + [rows,
                                pl.BlockSpec((1, N_EXPERTS, TM), lambda t: (t, 0, 0)),
                                pl.BlockSpec((1, N_EXPERTS, LANES), lambda t: (t, 0, 0))],
        out_shape=hmid_shapes + [jax.ShapeDtypeStruct((m, d), BF16),
                                 jax.ShapeDtypeStruct((nt, N_EXPERTS, TM), F32),
                                 jax.ShapeDtypeStruct((nt, N_EXPERTS, LANES), F32)],
        compiler_params=_cparams(1),
        name="moe_router",
    )(h2d, *attn_args, g, w_router_t)
    return tuple(outs) if attn2d is not None else (h2d, *outs)


def _segment_copies(t, seg_ref, len_ref, base_ref, make_copy, start):
    for e in range(N_EXPERTS):
        n = len_ref[t * N_EXPERTS + e]
        local0 = seg_ref[t * N_EXPERTS + e]
        global0 = base_ref[t * N_EXPERTS + e]
        off = 0
        for p in SEG_SIZES:
            take = (n & p) != 0

            @pl.when(take)
            def _(off=off, p=p):
                cp = make_copy(pl.multiple_of(local0 + off, SEG_ALIGN), pl.multiple_of(global0 + off, SEG_ALIGN), p)
                if start:
                    cp.start()
                else:
                    cp.wait()

            off = off + jnp.where(take, p, 0)


def _dispatch_kernel(seg_ref, len_ref, base_ref, hf_ref, route_ref, tri_ref, xin_ref, xs_ref, slot_ref,
                     stage_ref, sem):
    del xin_ref
    t = pl.program_id(0)
    route = route_ref[0]
    eidx = lax.broadcasted_iota(jnp.int32, route.shape, 0)
    sel1 = eidx == route[0:1, :].astype(jnp.int32)
    sel2 = eidx == route[1:2, :].astype(jnp.int32)
    sel = jnp.where(sel1, 1.0, 0.0) + jnp.where(sel2, 1.0, 0.0)
    sel16 = jnp.concatenate([sel, jnp.zeros_like(sel)], axis=0).astype(BF16)
    rank = _dot(sel16, tri_ref[...])[:N_EXPERTS]
    seg = jnp.zeros(route.shape, jnp.int32)
    for e in range(N_EXPERTS):
        seg = jnp.where(eidx == e, seg_ref[t * N_EXPERTS + e], seg)
    slot = seg.astype(F32) + rank
    slot1 = jnp.sum(jnp.where(sel1, slot, 0.0), axis=0, keepdims=True)
    slot2 = jnp.sum(jnp.where(sel2, slot, 0.0), axis=0, keepdims=True)
    slot_ref[0] = jnp.where(eidx == 0, slot1, jnp.where(eidx == 1, slot2, 0.0))
    r = lax.broadcasted_iota(jnp.int32, (TILE_ROWS, TM), 0)
    perm = jnp.where(r == slot1.astype(jnp.int32), 1.0, jnp.where(r == slot2.astype(jnp.int32), 1.0, 0.0))
    stage_ref[...] = _dot(perm.astype(BF16), hf_ref[...]).astype(BF16)

    def copy(local, glob, p):
        return pltpu.make_async_copy(stage_ref.at[pl.ds(local, p)], xs_ref.at[pl.ds(glob, p)], sem)

    _segment_copies(t, seg_ref, len_ref, base_ref, copy, start=True)
    _segment_copies(t, seg_ref, len_ref, base_ref, copy, start=False)


def _dispatch(seg, seg_len, base, hf, route, tri, x_zero):
    m, d = hf.shape
    nt = m // TM
    grid_spec = pltpu.PrefetchScalarGridSpec(
        num_scalar_prefetch=3,
        grid=(nt,),
        in_specs=[pl.BlockSpec((TM, d), lambda t, *_: (t, 0)),
                  pl.BlockSpec((1, N_EXPERTS, TM), lambda t, *_: (t, 0, 0)),
                  pl.BlockSpec((TM, TM), lambda t, *_: (0, 0)),
                  pl.BlockSpec(memory_space=pl.ANY)],
        out_specs=[pl.BlockSpec(memory_space=pl.ANY),
                   pl.BlockSpec((1, N_EXPERTS, TM), lambda t, *_: (t, 0, 0))],
        scratch_shapes=[pltpu.VMEM((TILE_ROWS, d), BF16), pltpu.SemaphoreType.DMA(())],
    )
    return pl.pallas_call(
        _dispatch_kernel,
        grid_spec=grid_spec,
        out_shape=[jax.ShapeDtypeStruct(x_zero.shape, BF16),
                   jax.ShapeDtypeStruct((nt, N_EXPERTS, TM), F32)],
        input_output_aliases={6: 0},
        compiler_params=_cparams(1),
        name="moe_dispatch",
    )(seg, seg_len, base, hf, route, tri, x_zero)


def _experts_kernel(te_ref, na_ref, x_ref, w1_ref, w3_ref, w2_ref, o_ref, acc_ref):
    del te_ref
    i = pl.program_id(0)
    f = pl.program_id(1)

    @pl.when(i < na_ref[0])
    def _():
        @pl.when(f == 0)
        def _():
            acc_ref[...] = jnp.zeros_like(acc_ref)

        x = x_ref[...]
        act = jax.nn.silu(_dot(x, w1_ref[0])) * _dot(x, w3_ref[0])
        acc_ref[...] += _dot(act.astype(BF16), w2_ref[0])

        @pl.when(f == pl.num_programs(1) - 1)
        def _():
            o_ref[...] = acc_ref[...].astype(BF16)

    @pl.when(jnp.logical_and(i >= na_ref[0], f == pl.num_programs(1) - 1))
    def _():
        o_ref[...] = jnp.zeros_like(o_ref)


def _experts(tile_expert, n_active, x_sorted, w1, w3, w2, layer, tf):
    rows, d = x_sorted.shape
    d_ff = w1.shape[3]
    n_tiles, nf = rows // TM, d_ff // tf
    row_blk = lambda i, f, te, na: (jnp.minimum(i, na[0] - 1), 0)
    chunk = lambda i, f, na: jnp.where(i < na[0], f, nf - 1)
    grid_spec = pltpu.PrefetchScalarGridSpec(
        num_scalar_prefetch=2,
        grid=(n_tiles, nf),
        in_specs=[pl.BlockSpec((TM, d), row_blk),
                  pl.BlockSpec((None, 1, d, tf), lambda i, f, te, na: (layer, te[i], 0, chunk(i, f, na))),
                  pl.BlockSpec((None, 1, d, tf), lambda i, f, te, na: (layer, te[i], 0, chunk(i, f, na))),
                  pl.BlockSpec((None, 1, tf, d), lambda i, f, te, na: (layer, te[i], chunk(i, f, na), 0))],
        out_specs=pl.BlockSpec((TM, d), lambda i, f, te, na: (i, 0)),
        scratch_shapes=[pltpu.VMEM((TM, d), F32)],
    )
    return pl.pallas_call(
        _experts_kernel,
        grid_spec=grid_spec,
        out_shape=jax.ShapeDtypeStruct((rows, d), BF16),
        compiler_params=_cparams(2),
        name="moe_experts",
    )(tile_expert, n_active, x_sorted, w1, w3, w2)


def _combine_kernel(seg_ref, len_ref, base_ref, h_ref, slot_ref, gate_ref, y_ref, o_ref, ybuf_ref, sem):
    t = pl.program_id(0)
    buf = t % 2

    def fetch(tile, b, start):
        def copy(local, glob, p):
            return pltpu.make_async_copy(y_ref.at[pl.ds(glob, p)], ybuf_ref.at[b, pl.ds(local, p)], sem.at[b])

        if start:
            ybuf_ref[b, 2 * TM:, :] = jnp.zeros((TILE_ROWS - 2 * TM, ybuf_ref.shape[2]), BF16)
        _segment_copies(tile, seg_ref, len_ref, base_ref, copy, start=start)

    @pl.when(t == 0)
    def _():
        fetch(t, buf, start=True)

    @pl.when(t + 1 < pl.num_programs(0))
    def _():
        fetch(t + 1, 1 - buf, start=True)

    fetch(t, buf, start=False)

    y = ybuf_ref[buf]
    lane = lax.broadcasted_iota(jnp.int32, (TM, TILE_ROWS), 1)
    out = h_ref[...]
    for k in range(2):
        onehot = jnp.where(lane == slot_ref[:, k:k + 1], 1.0, 0.0).astype(BF16)
        out = out + gate_ref[:, k:k + 1] * _dot(onehot, y)
    o_ref[...] = out


def _combine(seg, seg_len, base, h2d, slots, gates, y_sorted):
    m, d = h2d.shape
    nt = m // TM
    grid_spec = pltpu.PrefetchScalarGridSpec(
        num_scalar_prefetch=3,
        grid=(nt,),
        in_specs=[pl.BlockSpec((TM, d), lambda t, *_: (t, 0)),
                  pl.BlockSpec((TM, 2), lambda t, *_: (t, 0)),
                  pl.BlockSpec((TM, 2), lambda t, *_: (t, 0)),
                  pl.BlockSpec(memory_space=pl.ANY)],
        out_specs=pl.BlockSpec((TM, d), lambda t, *_: (t, 0)),
        scratch_shapes=[pltpu.VMEM((2, TILE_ROWS, d), BF16), pltpu.SemaphoreType.DMA((2,))],
    )
    return pl.pallas_call(
        _combine_kernel,
        grid_spec=grid_spec,
        out_shape=jax.ShapeDtypeStruct((m, d), F32),
        compiler_params=_cparams(1),
        name="moe_combine",
    )(seg, seg_len, base, h2d, slots, gates, y_sorted)


def _moe_tables(counts, n_tiles_max):
    pad = (counts + SEG_ALIGN - 1) // SEG_ALIGN * SEG_ALIGN
    seg = jnp.cumsum(pad, axis=1) - pad
    region = (jnp.sum(pad, axis=0) + TM - 1) // TM * TM
    base = (jnp.cumsum(region) - region)[None, :] + jnp.cumsum(pad, axis=0) - pad
    tiles_end = jnp.cumsum(region // TM)
    n_active = tiles_end[-1]
    tile = jnp.minimum(jnp.arange(n_tiles_max), n_active - 1)
    tile_expert = jnp.sum(tile[:, None] >= tiles_end[None, :], axis=1)
    flat = lambda a: a.reshape(-1).astype(jnp.int32)
    return flat(seg), flat(pad), flat(base), flat(tile_expert), flat(n_active)


def _moe_layer(h, attn, w_o, g, w_router, w1, w3, w2, layer, tf):
    bsz, t, d = h.shape
    m = bsz * t
    nt = m // TM
    n_tiles_max = -(-(nt * (TILE_ROWS - SEG_ALIGN) + N_EXPERTS * (TM - 1)) // TM)
    attn2d = None if attn is None else attn.reshape(m, attn.shape[2])
    h2d, hf, route, counts = _router(h.reshape(m, d), attn2d, w_o, g, w_router.T)
    seg, seg_len, base, tile_expert, n_active = _moe_tables(counts[:, :, 0].astype(jnp.int32), n_tiles_max)
    tri = (jnp.arange(TM)[:, None] < jnp.arange(TM)[None, :]).astype(BF16)
    x_sorted, slots = _dispatch(seg, seg_len, base, hf, route, tri, jnp.zeros((n_tiles_max * TM, d), BF16))
    y_sorted = _experts(tile_expert, n_active, x_sorted, w1, w3, w2, layer, tf)
    to_cols = lambda a: a.transpose(0, 2, 1).reshape(m, 2)
    out = _combine(seg, seg_len, base, h2d, to_cols(slots[:, :2, :]).astype(jnp.int32),
                   to_cols(route[:, 2:4, :]), y_sorted)
    return out.reshape(bsz, t, d)


def _kv_kernel(h_ref, g_ref, wd_ref, gl_ref, cs_ref, wk_ref, wv_ref, k_ref, v_ref):
    hs = _rms(h_ref[0], g_ref[...]).astype(BF16)
    ckv = _dot(hs, wd_ref[...])
    c_lat = _rms(ckv[:, :KV_LORA], gl_ref[...])
    pe = ckv[:, KV_LORA:KV_LORA + QK_ROPE]
    pe_rot = ckv[:, KV_LORA + QK_ROPE:]
    k_pe = pe * cs_ref[:, :QK_ROPE] + pe_rot * cs_ref[:, QK_ROPE:]
    lat = jnp.concatenate([c_lat, k_pe], axis=1).astype(BF16)
    k_ref[0] = _dot(lat, wk_ref[...]).astype(BF16)
    v_ref[0] = _dot(lat[:, :KV_LORA], wv_ref[...]).astype(BF16)


def _kv_proj(h, g_src, w_down_aug, g_latent, cs_k, w_k, w_v):
    bsz, t, d = h.shape
    nt = t // TT
    full = lambda shape: pl.BlockSpec(shape, lambda b, i: (0,) * len(shape))
    return pl.pallas_call(
        _kv_kernel,
        grid=(bsz, nt),
        in_specs=[pl.BlockSpec((1, TT, d), lambda b, i: (b, i, 0)), full((1, d)), full(w_down_aug.shape),
                  full((1, KV_LORA)), pl.BlockSpec((TT, 2 * QK_ROPE), lambda b, i: (i, 0)),
                  full(w_k.shape), full(w_v.shape)],
        out_specs=[pl.BlockSpec((1, TT, N_HEADS * HEAD_PAD), lambda b, i: (b, i, 0)),
                   pl.BlockSpec((1, TT, N_HEADS * V_HEAD), lambda b, i: (b, i, 0))],
        out_shape=[jax.ShapeDtypeStruct((bsz, t, N_HEADS * HEAD_PAD), BF16),
                   jax.ShapeDtypeStruct((bsz, t, N_HEADS * V_HEAD), BF16)],
        compiler_params=_cparams(2),
        name="kv_proj",
    )(h, g_src, w_down_aug, g_latent, cs_k, w_k, w_v)


def _q_kernel(h_ref, g_ref, wdq_ref, gq_ref, wuq_ref, ct_ref, st_ref, q_ref):
    hn = _rms(h_ref[0], g_ref[...]).astype(BF16)
    c_q = _rms(_dot(hn, wdq_ref[...]), gq_ref[...]).astype(BF16)
    q = _dot(c_q, wuq_ref[...])
    width = q.shape[1]
    ct = jnp.tile(ct_ref[...], (1, N_HEADS))
    st = jnp.tile(st_ref[...], (1, N_HEADS))
    q_ref[0] = (q * ct + pltpu.roll(q, width - QK_ROPE, axis=1) * st).astype(BF16)


def _q_proj(h, g, w_dq, g_q, w_uq_aug, ctab, stab):
    bsz, t, d = h.shape
    nt = t // TT
    full = lambda shape: pl.BlockSpec(shape, lambda b, i: (0,) * len(shape))
    tab = pl.BlockSpec((TT, HEAD_PAD), lambda b, i: (i, 0))
    return pl.pallas_call(
        _q_kernel,
        grid=(bsz, nt),
        in_specs=[pl.BlockSpec((1, TT, d), lambda b, i: (b, i, 0)), full((1, d)), full(w_dq.shape),
                  full((1, Q_LORA)), full(w_uq_aug.shape), tab, tab],
        out_specs=pl.BlockSpec((1, TT, N_HEADS * HEAD_PAD), lambda b, i: (b, i, 0)),
        out_shape=jax.ShapeDtypeStruct((bsz, t, N_HEADS * HEAD_PAD), BF16),
        compiler_params=_cparams(2),
        name="q_proj",
    )(h, g, w_dq, g_q, w_uq_aug, ctab, stab)


def _attn_kernel(q_ref, k_ref, v_ref, o_ref, s_ref):
    n_qblk = (q_ref.shape[1] - N_META) // Q_BLOCK
    heads = (0, 1)
    qk_sl = [slice(hh * HEAD_PAD, (hh + 1) * HEAD_PAD) for hh in heads]
    pair_v = 2 * V_HEAD

    def v_ext(vv):
        n = vv.shape[0]
        head0 = lax.broadcasted_iota(jnp.int32, (n, pair_v), 1) < V_HEAD
        sel0 = jnp.where(head0, 1.0, 0.0).astype(BF16)
        sel1 = jnp.where(head0, 0.0, 1.0).astype(BF16)
        r = lax.broadcasted_iota(jnp.int32, (2 * n, pair_v), 0)
        c = lax.broadcasted_iota(jnp.int32, (2 * n, pair_v), 1)
        ones = jnp.where(c == jnp.where(r < n, 0, 1), 1.0, 0.0).astype(BF16)
        return jnp.concatenate([jnp.concatenate([vv * sel0, vv * sel1], axis=0), ones], axis=1)

    def normalise(acc):
        head0 = lax.broadcasted_iota(jnp.int32, (acc.shape[0], pair_v), 1) < V_HEAD
        inv = jnp.where(head0, 1.0 / acc[:, pair_v:pair_v + 1], 1.0 / acc[:, pair_v + 1:pair_v + 2])
        return (acc[:, :pair_v] * inv).astype(BF16)

    k_meta = [k_ref[0, 0:N_META, qk_sl[hh]] for hh in heads]
    v_meta = v_ext(v_ref[0, 0:N_META, :])

    ps = []
    for hh in heads:
        s = _dot_nt(q_ref[0, 0:N_META, qk_sl[hh]], k_meta[hh])
        ps.append(jnp.exp(s - jnp.max(s, axis=1, keepdims=True)))
    o_ref[0, 0:N_META, :] = normalise(_dot(jnp.concatenate(ps, axis=1).astype(BF16), v_meta))

    rc = lax.broadcasted_iota(jnp.int32, (Q_BLOCK, Q_BLOCK), 0) // CHUNK
    cc = lax.broadcasted_iota(jnp.int32, (Q_BLOCK, Q_BLOCK), 1) // CHUNK
    diag_mask = cc <= rc

    def fold(s):
        tiles = [s[:, c:c + LANES] for c in range(0, s.shape[1], LANES)]
        return functools.reduce(jnp.maximum, tiles)

    for i in range(n_qblk):
        r0 = N_META + i * Q_BLOCK
        qs = [q_ref[0, r0:r0 + Q_BLOCK, qk_sl[hh]] for hh in heads]
        s_meta = [_dot_nt(qs[hh], k_meta[hh]) for hh in heads]
        mf = [None, None]
        for j in range(i + 1):
            c0 = N_META + j * Q_BLOCK
            for hh in heads:
                s = _dot_nt(qs[hh], k_ref[0, c0:c0 + Q_BLOCK, qk_sl[hh]])
                if j == i:
                    s = jnp.where(diag_mask, s, MASK_VALUE)
                s_ref[j, hh] = s
                mf[hh] = fold(s) if mf[hh] is None else jnp.maximum(mf[hh], fold(s))
        mb = []
        for hh in heads:
            m = jnp.maximum(jnp.max(mf[hh], axis=1, keepdims=True), jnp.max(s_meta[hh], axis=1, keepdims=True))
            mb.append(jnp.broadcast_to(m, (Q_BLOCK, LANES)))
        p_meta = [jnp.exp(s_meta[hh] - mb[hh][:, :N_META]).astype(BF16) for hh in heads]
        acc = _dot(jnp.concatenate(p_meta, axis=1), v_meta)
        for j in range(i + 1):
            c0 = N_META + j * Q_BLOCK
            p = [jnp.exp(s_ref[j, hh] - jnp.concatenate([mb[hh]] * (Q_BLOCK // LANES), axis=1)).astype(BF16)
                 for hh in heads]
            acc = acc + _dot(jnp.concatenate(p, axis=1), v_ext(v_ref[0, c0:c0 + Q_BLOCK, :]))
        o_ref[0, r0:r0 + Q_BLOCK, :] = normalise(acc)


def _attention(q, k, v):
    bsz, t, _ = q.shape
    qk_spec = pl.BlockSpec((1, t, 2 * HEAD_PAD), lambda b, p: (b, 0, p))
    v_spec = pl.BlockSpec((1, t, 2 * V_HEAD), lambda b, p: (b, 0, p))
    return pl.pallas_call(
        _attn_kernel,
        grid=(bsz, N_HEADS // 2),
        in_specs=[qk_spec, qk_spec, v_spec],
        out_specs=v_spec,
        out_shape=jax.ShapeDtypeStruct((bsz, t, N_HEADS * V_HEAD), BF16),
        scratch_shapes=[pltpu.VMEM(((t - N_META) // Q_BLOCK, 2, Q_BLOCK, Q_BLOCK), F32)],
        compiler_params=_cparams(2),
        name="attention",
    )(q, k, v)


def _final_kernel(h_ref, g_ref, o_ref):
    o_ref[0] = _rms(h_ref[0, N_META:, :], g_ref[...])


def _final_norm(h, g):
    bsz, t, d = h.shape
    return pl.pallas_call(
        _final_kernel,
        grid=(bsz,),
        in_specs=[pl.BlockSpec((1, t, d), lambda b: (b, 0, 0)), pl.BlockSpec((1, d), lambda b: (0, 0))],
        out_specs=pl.BlockSpec((1, t - N_META, d), lambda b: (b, 0, 0)),
        out_shape=jax.ShapeDtypeStruct((bsz, t - N_META, d), F32),
        compiler_params=_cparams(1),
        name="final_norm",
    )(h, g)


def _rot_cols(w):
    half = w.shape[1] // 2
    return jnp.concatenate([-w[:, half:], w[:, :half]], axis=1)


def _gate_pairs(w_a, w_i):
    def bd(w):
        z = jnp.zeros((RNN_BLOCK, RNN_BLOCK), w.dtype)
        return jnp.stack([jnp.block([[w[2 * p], z], [z, w[2 * p + 1]]]) for p in range(N_RNN_BLOCKS // 2)])
    return jnp.concatenate([bd(w_a), bd(w_i)], axis=2).astype(BF16)


def _q_up_aug(w_uq):
    w = w_uq.reshape(Q_LORA, N_HEADS, QK_NOPE + QK_ROPE)
    pe = w[:, :, QK_NOPE:]
    rot = jnp.concatenate([-pe[:, :, QK_ROPE // 2:], pe[:, :, :QK_ROPE // 2]], axis=2)
    return jnp.concatenate([w, rot], axis=2).reshape(Q_LORA, N_HEADS * HEAD_PAD).astype(BF16)


def _kv_up_aug(w_up):
    w = w_up.reshape(KV_LORA, N_HEADS, QK_NOPE + V_HEAD)
    pad = HEAD_PAD - QK_NOPE
    w_k_top = jnp.concatenate([w[:, :, :QK_NOPE], jnp.zeros((KV_LORA, N_HEADS, pad), w.dtype)], axis=2)
    eye = jnp.concatenate([jnp.zeros((QK_ROPE, QK_NOPE), w.dtype), jnp.eye(QK_ROPE, dtype=w.dtype),
                           jnp.zeros((QK_ROPE, pad - QK_ROPE), w.dtype)], axis=1)
    w_k_bot = jnp.broadcast_to(eye[:, None, :], (QK_ROPE, N_HEADS, HEAD_PAD))
    w_k = jnp.concatenate([w_k_top, w_k_bot], axis=0).reshape(KV_LORA + QK_ROPE, N_HEADS * HEAD_PAD)
    w_v = w[:, :, QK_NOPE:].reshape(KV_LORA, N_HEADS * V_HEAD)
    return w_k.astype(BF16), w_v.astype(BF16)


def _rope_tables(t):
    inv_freq = ROPE_THETA ** (-jnp.arange(0, QK_ROPE, 2, dtype=F32) / QK_ROPE)
    ang = jnp.arange(t, dtype=F32)[:, None] * inv_freq[None, :]
    cos = jnp.tile(jnp.cos(ang), (1, 2))
    sin = jnp.tile(jnp.sin(ang), (1, 2))
    cs_k = jnp.concatenate([cos, sin], axis=1)
    ones = jnp.ones((t, QK_NOPE), F32)
    zeros = jnp.zeros((t, HEAD_PAD - QK_NOPE - QK_ROPE), F32)
    ctab = ATTN_SCALE * jnp.concatenate([ones, cos, zeros], axis=1)
    stab = ATTN_SCALE * jnp.concatenate([0.0 * ones, sin, zeros], axis=1)
    return cs_k, ctab, stab


def kernel(x, meta_tokens, norm_mix, norm_ffn, norm_final, rnn_w_in, rnn_conv_w, rnn_conv_b, rnn_w_a, rnn_b_a, rnn_w_i, rnn_b_i, rnn_lambda, rnn_w_out, kv_norm_src, kv_w_down, kv_latent_norm, kv_w_up, q_w_down, q_latent_norm, q_w_up, attn_w_out, ffn_w1, ffn_w3, ffn_w2, moe_router, moe_w1, moe_w3, moe_w2):
    bsz = x.shape[0]
    meta = jnp.broadcast_to(meta_tokens.astype(x.dtype)[None], (bsz, N_META, x.shape[-1]))
    h = jnp.concatenate([meta, x], axis=1)
    t = h.shape[1]
    assert t % TT == 0 and (t - N_META) % Q_BLOCK == 0
    row = lambda v: v.reshape(1, -1)
    cs_k, ctab, stab = _rope_tables(t)

    ffn_w = [w.astype(BF16) for w in (ffn_w1, ffn_w3, ffn_w2)]
    moe_w = [w.astype(BF16) for w in (moe_w1, moe_w3, moe_w2)]
    k = v = attn = w_o = None
    for layer in range(DEPTH):
        if layer < N_A_LAYERS:
            a = layer
            h = _rnn_layer(h, row(norm_mix[layer]), rnn_w_in[a].astype(BF16), rnn_conv_w[a], row(rnn_conv_b[a]),
                           _gate_pairs(rnn_w_a[a], rnn_w_i[a]), row(rnn_b_a[a]), row(rnn_b_i[a]),
                           row(rnn_lambda[a]), rnn_w_out[a].astype(BF16))
        else:
            b = layer - N_A_LAYERS
            if b == 0:
                w_down_aug = jnp.concatenate([kv_w_down, _rot_cols(kv_w_down[:, KV_LORA:])], axis=1).astype(BF16)
                w_k, w_v = _kv_up_aug(kv_w_up)
                k, v = _kv_proj(h, row(kv_norm_src), w_down_aug, row(kv_latent_norm), cs_k, w_k, w_v)
            q = _q_proj(h, row(norm_mix[layer]), q_w_down[b].astype(BF16), row(q_latent_norm[b]),
                        _q_up_aug(q_w_up[b]), ctab, stab)
            attn, w_o = _attention(q, k, v), attn_w_out[b].astype(BF16)
        if layer % 2 == 0:
            h = _ffn_layer(h, attn, w_o, row(norm_ffn[layer]), *ffn_w, e=layer // 2)
        else:
            h = _moe_layer(h, attn, w_o, row(norm_ffn[layer]), moe_router[layer // 2], *moe_w,
                           layer=layer // 2, tf=1792)
        attn = None
    return _final_norm(h, row(norm_final))
```

```python
import functools
import math

import jax
import jax.numpy as jnp
from jax import lax
from jax.experimental import pallas as pl
from jax.experimental.pallas import tpu as pltpu

D_MODEL = 1024
N_META = 16
CHUNK = 64
NORM_EPS = 1e-6
DEPTH = 4
N_A_LAYERS = DEPTH // 2

D_RNN = D_MODEL
N_RNN_BLOCKS = 8
RNN_BLOCK = D_RNN // N_RNN_BLOCKS
CONV_WIDTH = 4
LRU_C = 8.0

N_HEADS = 16
QK_NOPE = 64
QK_ROPE = 32
V_HEAD = 64
Q_LORA = 384
KV_LORA = 256
ROPE_THETA = 10000.0
ATTN_SCALE = 1.0 / math.sqrt(QK_NOPE + QK_ROPE)
MASK_VALUE = -1e30

N_EXPERTS = 8

LANES = 128
SUBLANES = 8
VMEM_LIMIT = 56 * 1024 * 1024

HEAD_PAD = 128
Q_BLOCK = 256
TT = 688
FF_CHUNK = 1024
SCAN_GROUPS = 2
TM = 512
SEG_ALIGN = 16
SEG_SIZES = (512, 256, 128, 64, 32, 16)
TILE_ROWS = 2 * TM + N_EXPERTS * SEG_ALIGN
F32 = jnp.float32
BF16 = jnp.bfloat16


def _cparams(n_axes):
    return pltpu.CompilerParams(dimension_semantics=("arbitrary",) * n_axes,
                                vmem_limit_bytes=VMEM_LIMIT)


def _rms(x, g):
    return x * lax.rsqrt(jnp.mean(x * x, axis=-1, keepdims=True) + NORM_EPS) * g


def _dot(a, b):
    return jnp.dot(a, b, preferred_element_type=F32)


def _dot_nt(a, b):
    return lax.dot_general(a, b, (((1,), (1,)), ((), ())), preferred_element_type=F32)


def _rnn_kernel(h_ref, g_ref, win_ref, cw_ref, cb_ref, wg_ref, ba_ref, bi_ref, lam_ref, wout_ref,
                o_ref, xpad_ref, yg_ref, a_ref, u_ref, carry_ref):
    t = pl.program_id(1)
    tt = h_ref.shape[1]

    @pl.when(t == 0)
    def _():
        xpad_ref[0:SUBLANES, :] = jnp.zeros((SUBLANES, D_RNN), F32)
        carry_ref[...] = jnp.zeros_like(carry_ref)

    x = h_ref[0]
    hn = _rms(x, g_ref[...]).astype(BF16)
    yg_ref[...] = jax.nn.gelu(_dot(hn, win_ref[:, :D_RNN]), approximate=True)
    xpad_ref[SUBLANES:, :] = _dot(hn, win_ref[:, D_RNN:])

    xc = cb_ref[...] + cw_ref[CONV_WIDTH - 1:CONV_WIDTH, :] * xpad_ref[SUBLANES:, :]
    for j in range(CONV_WIDTH - 1):
        off = SUBLANES - (CONV_WIDTH - 1) + j
        xc = xc + cw_ref[j:j + 1, :] * xpad_ref[off:off + tt, :]
    xpad_ref[0:SUBLANES, :] = xpad_ref[tt:tt + SUBLANES, :]

    xcb = xc.astype(BF16)
    log_coef = -LRU_C * jax.nn.softplus(-lam_ref[...])
    pair = 2 * RNN_BLOCK
    for p in range(N_RNN_BLOCKS // 2):
        sl = slice(p * pair, (p + 1) * pair)
        gates = _dot(xcb[:, sl], wg_ref[p])
        gate_r = jax.nn.sigmoid(gates[:, :pair] + ba_ref[:, sl])
        gate_i = jax.nn.sigmoid(gates[:, pair:] + bi_ref[:, sl])
        log_a = log_coef[:, sl] * gate_r
        a_ref[:, sl] = jnp.exp(log_a)
        th = jnp.tanh(log_a)
        u_ref[:, sl] = jnp.sqrt(-2.0 * th / (1.0 - th)) * (gate_i * xc[:, sl])

    row = lax.broadcasted_iota(jnp.int32, (SUBLANES, D_RNN), 0)

    def groups(i, carry):
        local = []
        for g in range(SCAN_GROUPS):
            r0 = pl.multiple_of((i * SCAN_GROUPS + g) * SUBLANES, SUBLANES)
            a = a_ref[pl.ds(r0, SUBLANES), :]
            u = u_ref[pl.ds(r0, SUBLANES), :]
            for s in (1, 2, 4):
                keep = row >= s
                u = jnp.where(keep, a * pltpu.roll(u, s, axis=0) + u, u)
                a = jnp.where(keep, a * pltpu.roll(a, s, axis=0), a)
            local.append((r0, a, u))
        for r0, a, u in local:
            hs = a * carry + u
            u_ref[pl.ds(r0, SUBLANES), :] = hs
            carry = hs[SUBLANES - 1:SUBLANES, :]
        return carry

    carry_ref[...] = lax.fori_loop(0, tt // (SCAN_GROUPS * SUBLANES), groups, carry_ref[...])

    y = (u_ref[...] * yg_ref[...]).astype(BF16)
    o_ref[0] = x + _dot(y, wout_ref[...])


def _rnn_layer(h, g, w_in, conv_w, conv_b, w_gate, b_a, b_i, lam, w_out):
    bsz, t, d = h.shape
    nt = t // TT
    full = lambda shape: pl.BlockSpec(shape, lambda b, i: (0,) * len(shape))
    tile = pl.BlockSpec((1, TT, d), lambda b, i: (b, i, 0))
    return pl.pallas_call(
        _rnn_kernel,
        grid=(bsz, nt),
        in_specs=[tile, full((1, d)), full((d, 2 * D_RNN)), full((CONV_WIDTH, D_RNN)), full((1, D_RNN)),
                  full(w_gate.shape), full((1, D_RNN)), full((1, D_RNN)), full((1, D_RNN)),
                  full((D_RNN, d))],
        out_specs=tile,
        out_shape=jax.ShapeDtypeStruct(h.shape, F32),
        scratch_shapes=[pltpu.VMEM((TT + SUBLANES, D_RNN), F32), pltpu.VMEM((TT, D_RNN), F32),
                        pltpu.VMEM((TT, D_RNN), F32), pltpu.VMEM((TT, D_RNN), F32),
                        pltpu.VMEM((1, D_RNN), F32)],
        compiler_params=_cparams(2),
        name="rnn_mixer",
    )(h, g, w_in, conv_w, conv_b, w_gate, b_a, b_i, lam, w_out)


def _ffn_kernel(*refs, with_attn):
    if with_attn:
        h_ref, a_ref, wo_ref, g_ref, w1_ref, w3_ref, w2_ref, o_ref = refs
    else:
        h_ref, g_ref, w1_ref, w3_ref, w2_ref, o_ref = refs
    out = h_ref[0]
    if with_attn:
        out = out + _dot(a_ref[0], wo_ref[...])
    hf = _rms(out, g_ref[...]).astype(BF16)
    d_ff = w1_ref.shape[1]
    for c0 in range(0, d_ff, FF_CHUNK):
        c1 = min(c0 + FF_CHUNK, d_ff)
        act = jax.nn.silu(_dot(hf, w1_ref[:, c0:c1])) * _dot(hf, w3_ref[:, c0:c1])
        out = out + _dot(act.astype(BF16), w2_ref[c0:c1, :])
    o_ref[0] = out


def _ffn_layer(h, attn, w_o, g, w1, w3, w2, e):
    bsz, t, d = h.shape
    d_ff = w1.shape[2]
    nt = t // TT
    tile = pl.BlockSpec((1, TT, d), lambda b, i: (b, i, 0))
    resident = lambda shape, idx: pl.BlockSpec(shape, lambda b, i: idx, pipeline_mode=pl.Buffered(1))
    attn_args, attn_specs = (), []
    if attn is not None:
        attn_args = (attn, w_o)
        attn_specs = [pl.BlockSpec((1, TT, attn.shape[2]), lambda b, i: (b, i, 0)),
                      resident(w_o.shape, (0, 0))]
    return pl.pallas_call(
        functools.partial(_ffn_kernel, with_attn=attn is not None),
        grid=(bsz, nt),
        in_specs=[tile] + attn_specs +
                 [pl.BlockSpec((1, d), lambda b, i: (0, 0)),
                  resident((None, d, d_ff), (e, 0, 0)),
                  resident((None, d, d_ff), (e, 0, 0)),
                  resident((None, d_ff, d), (e, 0, 0))],
        out_specs=tile,
        out_shape=jax.ShapeDtypeStruct(h.shape, F32),
        compiler_params=_cparams(2),
        name="ffn_dense",
    )(h, *attn_args, g, w1, w3, w2)


def _router_kernel(*refs, with_attn):
    if with_attn:
        h_ref, a_ref, wo_ref, g_ref, wrt_ref, hmid_ref, hf_ref, route_ref, cnt_ref = refs
        hmid = h_ref[...] + _dot(a_ref[...], wo_ref[...])
        hmid_ref[...] = hmid
    else:
        h_ref, g_ref, wrt_ref, hf_ref, route_ref, cnt_ref = refs
        hmid = h_ref[...]
    hf = _rms(hmid, g_ref[...])
    hi = hf.astype(BF16)
    hf_ref[...] = hi
    lo = (hf - hi.astype(F32)).astype(BF16)
    w = wrt_ref[...]
    w_hi = w.astype(BF16)
    w_split = jnp.concatenate([w_hi, (w - w_hi.astype(F32)).astype(BF16)], axis=0)
    part_hi = _dot_nt(w_split, hi)
    logits = part_hi[:N_EXPERTS] + part_hi[N_EXPERTS:] + _dot_nt(w_split, lo)[:N_EXPERTS]
    eidx = lax.broadcasted_iota(jnp.int32, logits.shape, 0)
    m1 = jnp.max(logits, axis=0, keepdims=True)
    i1 = jnp.min(jnp.where(logits == m1, eidx, N_EXPERTS), axis=0, keepdims=True)
    rest = jnp.where(eidx == i1, -jnp.inf, logits)
    m2 = jnp.max(rest, axis=0, keepdims=True)
    i2 = jnp.min(jnp.where(rest == m2, eidx, N_EXPERTS), axis=0, keepdims=True)
    ex = jnp.exp(m2 - m1)
    inv = 1.0 / (1.0 + ex)
    sel = jnp.where(eidx == i1, 1.0, 0.0) + jnp.where(eidx == i2, 1.0, 0.0)
    cnt_ref[0] = jnp.broadcast_to(jnp.sum(sel, axis=1, keepdims=True), cnt_ref.shape[1:])
    route_ref[0] = jnp.where(eidx == 0, i1.astype(F32), jnp.where(eidx == 1, i2.astype(F32),
                             jnp.where(eidx == 2, inv, jnp.where(eidx == 3, ex * inv, 0.0))))


def _router(h2d, attn2d, w_o, g, w_router_t):
    m, d = h2d.shape
    nt = m // TM
    rows = pl.BlockSpec((TM, d), lambda t: (t, 0))
    attn_args, attn_specs, hmid_specs, hmid_shapes = (), [], [], []
    if attn2d is not None:
        attn_args = (attn2d, w_o)
        attn_specs = [pl.BlockSpec((TM, attn2d.shape[1]), lambda t: (t, 0)), pl.BlockSpec(w_o.shape, lambda t: (0, 0))]
        hmid_specs, hmid_shapes = [rows], [jax.ShapeDtypeStruct((m, d), F32)]
    outs = pl.pallas_call(
        functools.partial(_router_kernel, with_attn=attn2d is not None),
        grid=(nt,),
        in_specs=[rows] + attn_specs + [pl.BlockSpec((1, d), lambda t: (0, 0)),
                                        pl.BlockSpec((N_EXPERTS, d), lambda t: (0, 0))],
        out_specs=hmid_specs + [rows,
                                pl.BlockSpec((1, N_EXPERTS, TM), lambda t: (t, 0, 0)),
                                pl.BlockSpec((1, N_EXPERTS, LANES), lambda t: (t, 0, 0))],
        out_shape=hmid_shapes + [jax.ShapeDtypeStruct((m, d), BF16),
                                 jax.ShapeDtypeStruct((nt, N_EXPERTS, TM), F32),
                                 jax.ShapeDtypeStruct((nt, N_EXPERTS, LANES), F32)],
        compiler_params=_cparams(1),
        name="moe_router",
    )(h2d, *attn_args, g, w_router_t)
    return tuple(outs) if attn2d is not None else (h2d, *outs)


def _segment_copies(t, seg_ref, len_ref, base_ref, make_copy, start):
    for e in range(N_EXPERTS):
        n = len_ref[t * N_EXPERTS + e]
        local0 = seg_ref[t * N_EXPERTS + e]
        global0 = base_ref[t * N_EXPERTS + e]
        off = 0
        for p in SEG_SIZES:
            take = (n & p) != 0

            @pl.when(take)
            def _(off=off, p=p):
                cp = make_copy(pl.multiple_of(local0 + off, SEG_ALIGN), pl.multiple_of(global0 + off, SEG_ALIGN), p)
                if start:
                    cp.start()
                else:
                    cp.wait()

            off = off + jnp.where(take, p, 0)


def _dispatch_kernel(seg_ref, len_ref, base_ref, hf_ref, route_ref, tri_ref, xin_ref, xs_ref, slot_ref,
                     stage_ref, sem):
    del xin_ref
    t = pl.program_id(0)
    route = route_ref[0]
    eidx = lax.broadcasted_iota(jnp.int32, route.shape, 0)
    sel1 = eidx == route[0:1, :].astype(jnp.int32)
    sel2 = eidx == route[1:2, :].astype(jnp.int32)
    sel = jnp.where(sel1, 1.0, 0.0) + jnp.where(sel2, 1.0, 0.0)
    sel16 = jnp.concatenate([sel, jnp.zeros_like(sel)], axis=0).astype(BF16)
    rank = _dot(sel16, tri_ref[...])[:N_EXPERTS]
    seg = jnp.zeros(route.shape, jnp.int32)
    for e in range(N_EXPERTS):
        seg = jnp.where(eidx == e, seg_ref[t * N_EXPERTS + e], seg)
    slot = seg.astype(F32) + rank
    slot1 = jnp.sum(jnp.where(sel1, slot, 0.0), axis=0, keepdims=True)
    slot2 = jnp.sum(jnp.where(sel2, slot, 0.0), axis=0, keepdims=True)
    slot_ref[0] = jnp.where(eidx == 0, slot1, jnp.where(eidx == 1, slot2, 0.0))
    r = lax.broadcasted_iota(jnp.int32, (TILE_ROWS, TM), 0)
    perm = jnp.where(r == slot1.astype(jnp.int32), 1.0, jnp.where(r == slot2.astype(jnp.int32), 1.0, 0.0))
    buf = t % 2
    stage_ref[buf] = _dot(perm.astype(BF16), hf_ref[...]).astype(BF16)

    def store(tile, b, start):
        def copy(local, glob, p):
            return pltpu.make_async_copy(stage_ref.at[b, pl.ds(local, p)], xs_ref.at[pl.ds(glob, p)], sem.at[b])

        _segment_copies(tile, seg_ref, len_ref, base_ref, copy, start=start)

    store(t, buf, start=True)

    @pl.when(t > 0)
    def _():
        store(t - 1, 1 - buf, start=False)

    @pl.when(t == pl.num_programs(0) - 1)
    def _():
        store(t, buf, start=False)


def _dispatch(seg, seg_len, base, hf, route, tri, x_zero):
    m, d = hf.shape
    nt = m // TM
    grid_spec = pltpu.PrefetchScalarGridSpec(
        num_scalar_prefetch=3,
        grid=(nt,),
        in_specs=[pl.BlockSpec((TM, d), lambda t, *_: (t, 0)),
                  pl.BlockSpec((1, N_EXPERTS, TM), lambda t, *_: (t, 0, 0)),
                  pl.BlockSpec((TM, TM), lambda t, *_: (0, 0)),
                  pl.BlockSpec(memory_space=pl.ANY)],
        out_specs=[pl.BlockSpec(memory_space=pl.ANY),
                   pl.BlockSpec((1, N_EXPERTS, TM), lambda t, *_: (t, 0, 0))],
        scratch_shapes=[pltpu.VMEM((2, TILE_ROWS, d), BF16), pltpu.SemaphoreType.DMA((2,))],
    )
    return pl.pallas_call(
        _dispatch_kernel,
        grid_spec=grid_spec,
        out_shape=[jax.ShapeDtypeStruct(x_zero.shape, BF16),
                   jax.ShapeDtypeStruct((nt, N_EXPERTS, TM), F32)],
        input_output_aliases={6: 0},
        compiler_params=_cparams(1),
        name="moe_dispatch",
    )(seg, seg_len, base, hf, route, tri, x_zero)


def _experts_kernel(te_ref, na_ref, x_ref, w1_ref, w3_ref, w2_ref, o_ref, acc_ref):
    del te_ref
    i = pl.program_id(0)
    f = pl.program_id(1)

    @pl.when(i < na_ref[0])
    def _():
        @pl.when(f == 0)
        def _():
            acc_ref[...] = jnp.zeros_like(acc_ref)

        x = x_ref[...]
        act = jax.nn.silu(_dot(x, w1_ref[0])) * _dot(x, w3_ref[0])
        acc_ref[...] += _dot(act.astype(BF16), w2_ref[0])

        @pl.when(f == pl.num_programs(1) - 1)
        def _():
            o_ref[...] = acc_ref[...].astype(BF16)

    @pl.when(jnp.logical_and(i >= na_ref[0], f == pl.num_programs(1) - 1))
    def _():
        o_ref[...] = jnp.zeros_like(o_ref)


def _experts(tile_expert, n_active, x_sorted, w1, w3, w2, layer, tf):
    rows, d = x_sorted.shape
    d_ff = w1.shape[3]
    n_tiles, nf = rows // TM, d_ff // tf
    row_blk = lambda i, f, te, na: (jnp.minimum(i, na[0] - 1), 0)
    chunk = lambda i, f, na: jnp.where(i < na[0], f, nf - 1)
    grid_spec = pltpu.PrefetchScalarGridSpec(
        num_scalar_prefetch=2,
        grid=(n_tiles, nf),
        in_specs=[pl.BlockSpec((TM, d), row_blk),
                  pl.BlockSpec((None, 1, d, tf), lambda i, f, te, na: (layer, te[i], 0, chunk(i, f, na))),
                  pl.BlockSpec((None, 1, d, tf), lambda i, f, te, na: (layer, te[i], 0, chunk(i, f, na))),
                  pl.BlockSpec((None, 1, tf, d), lambda i, f, te, na: (layer, te[i], chunk(i, f, na), 0))],
        out_specs=pl.BlockSpec((TM, d), lambda i, f, te, na: (i, 0)),
        scratch_shapes=[pltpu.VMEM((TM, d), F32)],
    )
    return pl.pallas_call(
        _experts_kernel,
        grid_spec=grid_spec,
        out_shape=jax.ShapeDtypeStruct((rows, d), BF16),
        compiler_params=_cparams(2),
        name="moe_experts",
    )(tile_expert, n_active, x_sorted, w1, w3, w2)


def _combine_kernel(seg_ref, len_ref, base_ref, h_ref, slot_ref, gate_ref, y_ref, o_ref, ybuf_ref, sem):
    t = pl.program_id(0)
    buf = t % 2

    def fetch(tile, b, start):
        def copy(local, glob, p):
            return pltpu.make_async_copy(y_ref.at[pl.ds(glob, p)], ybuf_ref.at[b, pl.ds(local, p)], sem.at[b])

        if start:
            ybuf_ref[b, 2 * TM:, :] = jnp.zeros((TILE_ROWS - 2 * TM, ybuf_ref.shape[2]), BF16)
        _segment_copies(tile, seg_ref, len_ref, base_ref, copy, start=start)

    @pl.when(t == 0)
    def _():
        fetch(t, buf, start=True)

    @pl.when(t + 1 < pl.num_programs(0))
    def _():
        fetch(t + 1, 1 - buf, start=True)

    fetch(t, buf, start=False)

    y = ybuf_ref[buf]
    lane = lax.broadcasted_iota(jnp.int32, (TM, TILE_ROWS), 1)
    out = h_ref[...]
    for k in range(2):
        onehot = jnp.where(lane == slot_ref[:, k:k + 1], 1.0, 0.0).astype(BF16)
        out = out + gate_ref[:, k:k + 1] * _dot(onehot, y)
    o_ref[...] = out


def _combine(seg, seg_len, base, h2d, slots, gates, y_sorted):
    m, d = h2d.shape
    nt = m // TM
    grid_spec = pltpu.PrefetchScalarGridSpec(
        num_scalar_prefetch=3,
        grid=(nt,),
        in_specs=[pl.BlockSpec((TM, d), lambda t, *_: (t, 0)),
                  pl.BlockSpec((TM, 2), lambda t, *_: (t, 0)),
                  pl.BlockSpec((TM, 2), lambda t, *_: (t, 0)),
                  pl.BlockSpec(memory_space=pl.ANY)],
        out_specs=pl.BlockSpec((TM, d), lambda t, *_: (t, 0)),
        scratch_shapes=[pltpu.VMEM((2, TILE_ROWS, d), BF16), pltpu.SemaphoreType.DMA((2,))],
    )
    return pl.pallas_call(
        _combine_kernel,
        grid_spec=grid_spec,
        out_shape=jax.ShapeDtypeStruct((m, d), F32),
        compiler_params=_cparams(1),
        name="moe_combine",
    )(seg, seg_len, base, h2d, slots, gates, y_sorted)


def _moe_tables(counts, n_tiles_max):
    pad = (counts + SEG_ALIGN - 1) // SEG_ALIGN * SEG_ALIGN
    seg = jnp.cumsum(pad, axis=1) - pad
    region = (jnp.sum(pad, axis=0) + TM - 1) // TM * TM
    base = (jnp.cumsum(region) - region)[None, :] + jnp.cumsum(pad, axis=0) - pad
    tiles_end = jnp.cumsum(region // TM)
    n_active = tiles_end[-1]
    tile = jnp.minimum(jnp.arange(n_tiles_max), n_active - 1)
    tile_expert = jnp.sum(tile[:, None] >= tiles_end[None, :], axis=1)
    flat = lambda a: a.reshape(-1).astype(jnp.int32)
    return flat(seg), flat(pad), flat(base), flat(tile_expert), flat(n_active)


def _moe_layer(h, attn, w_o, g, w_router, w1, w3, w2, layer, tf):
    bsz, t, d = h.shape
    m = bsz * t
    nt = m // TM
    n_tiles_max = -(-(nt * (TILE_ROWS - SEG_ALIGN) + N_EXPERTS * (TM - 1)) // TM)
    attn2d = None if attn is None else attn.reshape(m, attn.shape[2])
    h2d, hf, route, counts = _router(h.reshape(m, d), attn2d, w_o, g, w_router.T)
    seg, seg_len, base, tile_expert, n_active = _moe_tables(counts[:, :, 0].astype(jnp.int32), n_tiles_max)
    tri = (jnp.arange(TM)[:, None] < jnp.arange(TM)[None, :]).astype(BF16)
    x_sorted, slots = _dispatch(seg, seg_len, base, hf, route, tri, jnp.zeros((n_tiles_max * TM, d), BF16))
    y_sorted = _experts(tile_expert, n_active, x_sorted, w1, w3, w2, layer, tf)
    to_cols = lambda a: a.transpose(0, 2, 1).reshape(m, 2)
    out = _combine(seg, seg_len, base, h2d, to_cols(slots[:, :2, :]).astype(jnp.int32),
                   to_cols(route[:, 2:4, :]), y_sorted)
    return out.reshape(bsz, t, d)


def _kv_kernel(h_ref, g_ref, wd_ref, gl_ref, cs_ref, wk_ref, wv_ref, k_ref, v_ref):
    hs = _rms(h_ref[0], g_ref[...]).astype(BF16)
    ckv = _dot(hs, wd_ref[...])
    c_lat = _rms(ckv[:, :KV_LORA], gl_ref[...])
    pe = ckv[:, KV_LORA:KV_LORA + QK_ROPE]
    pe_rot = ckv[:, KV_LORA + QK_ROPE:]
    k_pe = pe * cs_ref[:, :QK_ROPE] + pe_rot * cs_ref[:, QK_ROPE:]
    lat = jnp.concatenate([c_lat, k_pe], axis=1).astype(BF16)
    k_ref[0] = _dot(lat, wk_ref[...]).astype(BF16)
    v_ref[0] = _dot(lat[:, :KV_LORA], wv_ref[...]).astype(BF16)


def _kv_proj(h, g_src, w_down_aug, g_latent, cs_k, w_k, w_v):
    bsz, t, d = h.shape
    nt = t // TT
    full = lambda shape: pl.BlockSpec(shape, lambda b, i: (0,) * len(shape))
    return pl.pallas_call(
        _kv_kernel,
        grid=(bsz, nt),
        in_specs=[pl.BlockSpec((1, TT, d), lambda b, i: (b, i, 0)), full((1, d)), full(w_down_aug.shape),
                  full((1, KV_LORA)), pl.BlockSpec((TT, 2 * QK_ROPE), lambda b, i: (i, 0)),
                  full(w_k.shape), full(w_v.shape)],
        out_specs=[pl.BlockSpec((1, TT, N_HEADS * HEAD_PAD), lambda b, i: (b, i, 0)),
                   pl.BlockSpec((1, TT, N_HEADS * V_HEAD), lambda b, i: (b, i, 0))],
        out_shape=[jax.ShapeDtypeStruct((bsz, t, N_HEADS * HEAD_PAD), BF16),
                   jax.ShapeDtypeStruct((bsz, t, N_HEADS * V_HEAD), BF16)],
        compiler_params=_cparams(2),
        name="kv_proj",
    )(h, g_src, w_down_aug, g_latent, cs_k, w_k, w_v)


def _q_kernel(h_ref, g_ref, wdq_ref, gq_ref, wuq_ref, ct_ref, st_ref, q_ref):
    hn = _rms(h_ref[0], g_ref[...]).astype(BF16)
    c_q = _rms(_dot(hn, wdq_ref[...]), gq_ref[...]).astype(BF16)
    q = _dot(c_q, wuq_ref[...])
    width = q.shape[1]
    ct = jnp.tile(ct_ref[...], (1, N_HEADS))
    st = jnp.tile(st_ref[...], (1, N_HEADS))
    q_ref[0] = (q * ct + pltpu.roll(q, width - QK_ROPE, axis=1) * st).astype(BF16)


def _q_proj(h, g, w_dq, g_q, w_uq_aug, ctab, stab):
    bsz, t, d = h.shape
    nt = t // TT
    full = lambda shape: pl.BlockSpec(shape, lambda b, i: (0,) * len(shape))
    tab = pl.BlockSpec((TT, HEAD_PAD), lambda b, i: (i, 0))
    return pl.pallas_call(
        _q_kernel,
        grid=(bsz, nt),
        in_specs=[pl.BlockSpec((1, TT, d), lambda b, i: (b, i, 0)), full((1, d)), full(w_dq.shape),
                  full((1, Q_LORA)), full(w_uq_aug.shape), tab, tab],
        out_specs=pl.BlockSpec((1, TT, N_HEADS * HEAD_PAD), lambda b, i: (b, i, 0)),
        out_shape=jax.ShapeDtypeStruct((bsz, t, N_HEADS * HEAD_PAD), BF16),
        compiler_params=_cparams(2),
        name="q_proj",
    )(h, g, w_dq, g_q, w_uq_aug, ctab, stab)


def _attn_kernel(q_ref, k_ref, v_ref, o_ref, s_ref, vt_ref):
    n_qblk = (q_ref.shape[1] - N_META) // Q_BLOCK
    heads = (0, 1)
    qk_sl = [slice(hh * HEAD_PAD, (hh + 1) * HEAD_PAD) for hh in heads]
    pair_v = 2 * V_HEAD

    def v_ext(vv):
        n = vv.shape[0]
        head0 = lax.broadcasted_iota(jnp.int32, (n, pair_v), 1) < V_HEAD
        sel0 = jnp.where(head0, 1.0, 0.0).astype(BF16)
        sel1 = jnp.where(head0, 0.0, 1.0).astype(BF16)
        r = lax.broadcasted_iota(jnp.int32, (2 * n, pair_v), 0)
        c = lax.broadcasted_iota(jnp.int32, (2 * n, pair_v), 1)
        ones = jnp.where(c == jnp.where(r < n, 0, 1), 1.0, 0.0).astype(BF16)
        return jnp.concatenate([jnp.concatenate([vv * sel0, vv * sel1], axis=0), ones], axis=1)

    def normalise(acc):
        head0 = lax.broadcasted_iota(jnp.int32, (acc.shape[0], pair_v), 1) < V_HEAD
        inv = jnp.where(head0, 1.0 / acc[:, pair_v:pair_v + 1], 1.0 / acc[:, pair_v + 1:pair_v + 2])
        return (acc[:, :pair_v] * inv).astype(BF16)

    k_meta = [k_ref[0, 0:N_META, qk_sl[hh]] for hh in heads]
    v_meta = v_ext(v_ref[0, 0:N_META, :])

    ps = []
    for hh in heads:
        s = _dot_nt(q_ref[0, 0:N_META, qk_sl[hh]], k_meta[hh])
        ps.append(jnp.exp(s - jnp.max(s, axis=1, keepdims=True)))
    o_ref[0, 0:N_META, :] = normalise(_dot(jnp.concatenate(ps, axis=1).astype(BF16), v_meta))

    kc = lax.broadcasted_iota(jnp.int32, (Q_BLOCK, Q_BLOCK), 0) // CHUNK
    qc = lax.broadcasted_iota(jnp.int32, (Q_BLOCK, Q_BLOCK), 1) // CHUNK
    diag_mask = kc <= qc
    eye = jnp.where(lax.broadcasted_iota(jnp.int32, (pair_v, pair_v), 0)
                    == lax.broadcasted_iota(jnp.int32, (pair_v, pair_v), 1), 1.0, 0.0).astype(BF16)

    def vt_ext(vv):
        vt = _dot_nt(eye, vv).astype(BF16)
        ones = jnp.ones((SEG_ALIGN, vv.shape[0]), BF16)
        return [jnp.concatenate([vt[hh * V_HEAD:(hh + 1) * V_HEAD], ones], axis=0) for hh in heads]

    vt_meta = vt_ext(v_ref[0, 0:N_META, :])
    n_kblk = n_qblk
    for j in range(n_kblk):
        c0 = N_META + j * Q_BLOCK
        blk = vt_ext(v_ref[0, c0:c0 + Q_BLOCK, :])
        for hh in heads:
            vt_ref[j, hh] = blk[hh]

    def fold(s):
        return jnp.max(s.reshape(s.shape[0] // SUBLANES, SUBLANES, s.shape[1]), axis=0)

    for i in range(n_qblk):
        r0 = N_META + i * Q_BLOCK
        qs = [q_ref[0, r0:r0 + Q_BLOCK, qk_sl[hh]] for hh in heads]
        s_meta = [_dot_nt(k_meta[hh], qs[hh]) for hh in heads]
        mf = [None, None]
        for j in range(i + 1):
            c0 = N_META + j * Q_BLOCK
            for hh in heads:
                s = _dot_nt(k_ref[0, c0:c0 + Q_BLOCK, qk_sl[hh]], qs[hh])
                if j == i:
                    s = jnp.where(diag_mask, s, MASK_VALUE)
                s_ref[j, hh] = s
                mf[hh] = fold(s) if mf[hh] is None else jnp.maximum(mf[hh], fold(s))
        outs = []
        for hh in heads:
            m = jnp.maximum(jnp.max(mf[hh], axis=0, keepdims=True), jnp.max(s_meta[hh], axis=0, keepdims=True))
            acc = _dot(vt_meta[hh], jnp.exp(s_meta[hh] - m).astype(BF16))
            for j in range(i + 1):
                acc = acc + _dot(vt_ref[j, hh], jnp.exp(s_ref[j, hh] - m).astype(BF16))
            outs.append(acc[:V_HEAD] * (1.0 / acc[V_HEAD:V_HEAD + 1]))
        o_ref[0, r0:r0 + Q_BLOCK, :] = jnp.concatenate(outs, axis=0).T.astype(BF16)


def _attention(q, k, v):
    bsz, t, _ = q.shape
    n_blk = (t - N_META) // Q_BLOCK
    qk_spec = pl.BlockSpec((1, t, 2 * HEAD_PAD), lambda b, p: (b, 0, p))
    v_spec = pl.BlockSpec((1, t, 2 * V_HEAD), lambda b, p: (b, 0, p))
    return pl.pallas_call(
        _attn_kernel,
        grid=(bsz, N_HEADS // 2),
        in_specs=[qk_spec, qk_spec, v_spec],
        out_specs=v_spec,
        out_shape=jax.ShapeDtypeStruct((bsz, t, N_HEADS * V_HEAD), BF16),
        scratch_shapes=[pltpu.VMEM((n_blk, 2, Q_BLOCK, Q_BLOCK), F32),
                        pltpu.VMEM((n_blk, 2, V_HEAD + SEG_ALIGN, Q_BLOCK), BF16)],
        compiler_params=_cparams(2),
        name="attention",
    )(q, k, v)


def _final_kernel(h_ref, g_ref, o_ref):
    o_ref[0] = _rms(h_ref[0, N_META:, :], g_ref[...])


def _final_norm(h, g):
    bsz, t, d = h.shape
    return pl.pallas_call(
        _final_kernel,
        grid=(bsz,),
        in_specs=[pl.BlockSpec((1, t, d), lambda b: (b, 0, 0)), pl.BlockSpec((1, d), lambda b: (0, 0))],
        out_specs=pl.BlockSpec((1, t - N_META, d), lambda b: (b, 0, 0)),
        out_shape=jax.ShapeDtypeStruct((bsz, t - N_META, d), F32),
        compiler_params=_cparams(1),
        name="final_norm",
    )(h, g)


def _rot_cols(w):
    half = w.shape[1] // 2
    return jnp.concatenate([-w[:, half:], w[:, :half]], axis=1)


def _gate_pairs(w_a, w_i):
    def bd(w):
        z = jnp.zeros((RNN_BLOCK, RNN_BLOCK), w.dtype)
        return jnp.stack([jnp.block([[w[2 * p], z], [z, w[2 * p + 1]]]) for p in range(N_RNN_BLOCKS // 2)])
    return jnp.concatenate([bd(w_a), bd(w_i)], axis=2).astype(BF16)


def _q_up_aug(w_uq):
    w = w_uq.reshape(Q_LORA, N_HEADS, QK_NOPE + QK_ROPE)
    pe = w[:, :, QK_NOPE:]
    rot = jnp.concatenate([-pe[:, :, QK_ROPE // 2:], pe[:, :, :QK_ROPE // 2]], axis=2)
    return jnp.concatenate([w, rot], axis=2).reshape(Q_LORA, N_HEADS * HEAD_PAD).astype(BF16)


def _kv_up_aug(w_up):
    w = w_up.reshape(KV_LORA, N_HEADS, QK_NOPE + V_HEAD)
    pad = HEAD_PAD - QK_NOPE
    w_k_top = jnp.concatenate([w[:, :, :QK_NOPE], jnp.zeros((KV_LORA, N_HEADS, pad), w.dtype)], axis=2)
    eye = jnp.concatenate([jnp.zeros((QK_ROPE, QK_NOPE), w.dtype), jnp.eye(QK_ROPE, dtype=w.dtype),
                           jnp.zeros((QK_ROPE, pad - QK_ROPE), w.dtype)], axis=1)
    w_k_bot = jnp.broadcast_to(eye[:, None, :], (QK_ROPE, N_HEADS, HEAD_PAD))
    w_k = jnp.concatenate([w_k_top, w_k_bot], axis=0).reshape(KV_LORA + QK_ROPE, N_HEADS * HEAD_PAD)
    w_v = w[:, :, QK_NOPE:].reshape(KV_LORA, N_HEADS * V_HEAD)
    return w_k.astype(BF16), w_v.astype(BF16)


def _rope_tables(t):
    inv_freq = ROPE_THETA ** (-jnp.arange(0, QK_ROPE, 2, dtype=F32) / QK_ROPE)
    ang = jnp.arange(t, dtype=F32)[:, None] * inv_freq[None, :]
    cos = jnp.tile(jnp.cos(ang), (1, 2))
    sin = jnp.tile(jnp.sin(ang), (1, 2))
    cs_k = jnp.concatenate([cos, sin], axis=1)
    ones = jnp.ones((t, QK_NOPE), F32)
    zeros = jnp.zeros((t, HEAD_PAD - QK_NOPE - QK_ROPE), F32)
    ctab = ATTN_SCALE * jnp.concatenate([ones, cos, zeros], axis=1)
    stab = ATTN_SCALE * jnp.concatenate([0.0 * ones, sin, zeros], axis=1)
    return cs_k, ctab, stab


def kernel(x, meta_tokens, norm_mix, norm_ffn, norm_final, rnn_w_in, rnn_conv_w, rnn_conv_b, rnn_w_a, rnn_b_a, rnn_w_i, rnn_b_i, rnn_lambda, rnn_w_out, kv_norm_src, kv_w_down, kv_latent_norm, kv_w_up, q_w_down, q_latent_norm, q_w_up, attn_w_out, ffn_w1, ffn_w3, ffn_w2, moe_router, moe_w1, moe_w3, moe_w2):
    bsz = x.shape[0]
    meta = jnp.broadcast_to(meta_tokens.astype(x.dtype)[None], (bsz, N_META, x.shape[-1]))
    h = jnp.concatenate([meta, x], axis=1)
    t = h.shape[1]
    assert t % TT == 0 and (t - N_META) % Q_BLOCK == 0
    row = lambda v: v.reshape(1, -1)
    cs_k, ctab, stab = _rope_tables(t)

    ffn_w = [w.astype(BF16) for w in (ffn_w1, ffn_w3, ffn_w2)]
    moe_w = [w.astype(BF16) for w in (moe_w1, moe_w3, moe_w2)]
    k = v = attn = w_o = None
    for layer in range(DEPTH):
        if layer < N_A_LAYERS:
            a = layer
            h = _rnn_layer(h, row(norm_mix[layer]), rnn_w_in[a].astype(BF16), rnn_conv_w[a], row(rnn_conv_b[a]),
                           _gate_pairs(rnn_w_a[a], rnn_w_i[a]), row(rnn_b_a[a]), row(rnn_b_i[a]),
                           row(rnn_lambda[a]), rnn_w_out[a].astype(BF16))
        else:
            b = layer - N_A_LAYERS
            if b == 0:
                w_down_aug = jnp.concatenate([kv_w_down, _rot_cols(kv_w_down[:, KV_LORA:])], axis=1).astype(BF16)
                w_k, w_v = _kv_up_aug(kv_w_up)
                k, v = _kv_proj(h, row(kv_norm_src), w_down_aug, row(kv_latent_norm), cs_k, w_k, w_v)
            q = _q_proj(h, row(norm_mix[layer]), q_w_down[b].astype(BF16), row(q_latent_norm[b]),
                        _q_up_aug(q_w_up[b]), ctab, stab)
            attn, w_o = _attention(q, k, v), attn_w_out[b].astype(BF16)
        if layer % 2 == 0:
            h = _ffn_layer(h, attn, w_o, row(norm_ffn[layer]), *ffn_w, e=layer // 2)
        else:
            h = _moe_layer(h, attn, w_o, row(norm_ffn[layer]), moe_router[layer // 2], *moe_w,
                           layer=layer // 2, tf=1792)
        attn = None
    return _final_norm(h, row(norm_final))
```

```python
import functools
import math

import jax
import jax.numpy as jnp
from jax import lax
from jax.experimental import pallas as pl
from jax.experimental.pallas import tpu as pltpu

D_MODEL = 1024
N_META = 16
CHUNK = 64
NORM_EPS = 1e-6
DEPTH = 4
N_A_LAYERS = DEPTH // 2

D_RNN = D_MODEL
N_RNN_BLOCKS = 8
RNN_BLOCK = D_RNN // N_RNN_BLOCKS
CONV_WIDTH = 4
LRU_C = 8.0

N_HEADS = 16
QK_NOPE = 64
QK_ROPE = 32
V_HEAD = 64
Q_LORA = 384
KV_LORA = 256
ROPE_THETA = 10000.0
ATTN_SCALE = 1.0 / math.sqrt(QK_NOPE + QK_ROPE)
MASK_VALUE = -1e30
SCORE_SCALE = ATTN_SCALE * math.log2(math.e)

N_EXPERTS = 8

LANES = 128
SUBLANES = 8
VMEM_LIMIT = 56 * 1024 * 1024

HEAD_PAD = 128
Q_BLOCK = 256
TT = 688
FF_CHUNK = 1024
SCAN_GROUPS = 2
TM = 512
SEG_ALIGN = 16
SEG_SIZES = (512, 256, 128, 64, 32, 16)
TILE_ROWS = 2 * TM + N_EXPERTS * SEG_ALIGN
F32 = jnp.float32
BF16 = jnp.bfloat16


def _cparams(n_axes):
    return pltpu.CompilerParams(dimension_semantics=("arbitrary",) * n_axes,
                                vmem_limit_bytes=VMEM_LIMIT)


def _rms(x, g):
    return x * lax.rsqrt(jnp.mean(x * x, axis=-1, keepdims=True) + NORM_EPS) * g


def _sigmoid(x):
    return 0.5 * jnp.tanh(0.5 * x) + 0.5


def _dot(a, b):
    return jnp.dot(a, b, preferred_element_type=F32)


def _dot_nt(a, b):
    return lax.dot_general(a, b, (((1,), (1,)), ((), ())), preferred_element_type=F32)


def _rnn_kernel(h_ref, g_ref, win_ref, cw_ref, cb_ref, wg_ref, ba_ref, bi_ref, lam_ref, wout_ref,
                o_ref, xpad_ref, yg_ref, a_ref, u_ref, carry_ref):
    t = pl.program_id(1)
    tt = h_ref.shape[1]

    @pl.when(t == 0)
    def _():
        xpad_ref[0:SUBLANES, :] = jnp.zeros((SUBLANES, D_RNN), F32)
        carry_ref[...] = jnp.zeros_like(carry_ref)

    x = h_ref[0]
    hn = _rms(x, g_ref[...]).astype(BF16)
    yg_ref[...] = jax.nn.gelu(_dot(hn, win_ref[:, :D_RNN]), approximate=True)
    xpad_ref[SUBLANES:, :] = _dot(hn, win_ref[:, D_RNN:])

    xc = cb_ref[...] + cw_ref[CONV_WIDTH - 1:CONV_WIDTH, :] * xpad_ref[SUBLANES:, :]
    for j in range(CONV_WIDTH - 1):
        off = SUBLANES - (CONV_WIDTH - 1) + j
        xc = xc + cw_ref[j:j + 1, :] * xpad_ref[off:off + tt, :]
    xpad_ref[0:SUBLANES, :] = xpad_ref[tt:tt + SUBLANES, :]

    xcb = xc.astype(BF16)
    log_coef = -LRU_C * jax.nn.softplus(-lam_ref[...])
    pair = 2 * RNN_BLOCK
    for p in range(N_RNN_BLOCKS // 2):
        sl = slice(p * pair, (p + 1) * pair)
        gates = _dot(xcb[:, sl], wg_ref[p])
        gate_r = _sigmoid(gates[:, :pair] + ba_ref[:, sl])
        gate_i = _sigmoid(gates[:, pair:] + bi_ref[:, sl])
        log_a = log_coef[:, sl] * gate_r
        a_ref[:, sl] = jnp.exp(log_a)
        th = jnp.tanh(log_a)
        u_ref[:, sl] = jnp.sqrt(-2.0 * th / (1.0 - th)) * (gate_i * xc[:, sl])

    row = lax.broadcasted_iota(jnp.int32, (SUBLANES, D_RNN), 0)

    def groups(i, carry):
        local = []
        for g in range(SCAN_GROUPS):
            r0 = pl.multiple_of((i * SCAN_GROUPS + g) * SUBLANES, SUBLANES)
            a = a_ref[pl.ds(r0, SUBLANES), :]
            u = u_ref[pl.ds(r0, SUBLANES), :]
            for s in (1, 2, 4):
                keep = row >= s
                u = jnp.where(keep, a * pltpu.roll(u, s, axis=0) + u, u)
                a = jnp.where(keep, a * pltpu.roll(a, s, axis=0), a)
            local.append((r0, a, u))
        for r0, a, u in local:
            hs = a * carry + u
            u_ref[pl.ds(r0, SUBLANES), :] = hs
            carry = hs[SUBLANES - 1:SUBLANES, :]
        return carry

    carry_ref[...] = lax.fori_loop(0, tt // (SCAN_GROUPS * SUBLANES), groups, carry_ref[...])

    y = (u_ref[...] * yg_ref[...]).astype(BF16)
    o_ref[0] = x + _dot(y, wout_ref[...])


def _rnn_layer(h, g, w_in, conv_w, conv_b, w_gate, b_a, b_i, lam, w_out):
    bsz, t, d = h.shape
    nt = t // TT
    full = lambda shape: pl.BlockSpec(shape, lambda b, i: (0,) * len(shape))
    tile = pl.BlockSpec((1, TT, d), lambda b, i: (b, i, 0))
    return pl.pallas_call(
        _rnn_kernel,
        grid=(bsz, nt),
        in_specs=[tile, full((1, d)), full((d, 2 * D_RNN)), full((CONV_WIDTH, D_RNN)), full((1, D_RNN)),
                  full(w_gate.shape), full((1, D_RNN)), full((1, D_RNN)), full((1, D_RNN)),
                  full((D_RNN, d))],
        out_specs=tile,
        out_shape=jax.ShapeDtypeStruct(h.shape, F32),
        scratch_shapes=[pltpu.VMEM((TT + SUBLANES, D_RNN), F32), pltpu.VMEM((TT, D_RNN), F32),
                        pltpu.VMEM((TT, D_RNN), F32), pltpu.VMEM((TT, D_RNN), F32),
                        pltpu.VMEM((1, D_RNN), F32)],
        compiler_params=_cparams(2),
        name="rnn_mixer",
    )(h, g, w_in, conv_w, conv_b, w_gate, b_a, b_i, lam, w_out)


def _ffn_kernel(*refs, with_attn):
    if with_attn:
        h_ref, a_ref, wo_ref, g_ref, w1_ref, w3_ref, w2_ref, o_ref = refs
    else:
        h_ref, g_ref, w1_ref, w3_ref, w2_ref, o_ref = refs
    out = h_ref[0]
    if with_attn:
        out = out + _dot(a_ref[0], wo_ref[...])
    hf = _rms(out, g_ref[...]).astype(BF16)
    d_ff = w1_ref.shape[1]
    for c0 in range(0, d_ff, FF_CHUNK):
        c1 = min(c0 + FF_CHUNK, d_ff)
        act = jax.nn.silu(_dot(hf, w1_ref[:, c0:c1])) * _dot(hf, w3_ref[:, c0:c1])
        out = out + _dot(act.astype(BF16), w2_ref[c0:c1, :])
    o_ref[0] = out


def _ffn_layer(h, attn, w_o, g, w1, w3, w2, e):
    bsz, t, d = h.shape
    d_ff = w1.shape[2]
    nt = t // TT
    tile = pl.BlockSpec((1, TT, d), lambda b, i: (b, i, 0))
    resident = lambda shape, idx: pl.BlockSpec(shape, lambda b, i: idx, pipeline_mode=pl.Buffered(1))
    attn_args, attn_specs = (), []
    if attn is not None:
        attn_args = (attn, w_o)
        attn_specs = [pl.BlockSpec((1, TT, attn.shape[2]), lambda b, i: (b, i, 0)),
                      resident(w_o.shape, (0, 0))]
    return pl.pallas_call(
        functools.partial(_ffn_kernel, with_attn=attn is not None),
        grid=(bsz, nt),
        in_specs=[tile] + attn_specs +
                 [pl.BlockSpec((1, d), lambda b, i: (0, 0)),
                  resident((None, d, d_ff), (e, 0, 0)),
                  resident((None, d, d_ff), (e, 0, 0)),
                  resident((None, d_ff, d), (e, 0, 0))],
        out_specs=tile,
        out_shape=jax.ShapeDtypeStruct(h.shape, F32),
        compiler_params=_cparams(2),
        name="ffn_dense",
    )(h, *attn_args, g, w1, w3, w2)


def _router_kernel(*refs, with_attn):
    if with_attn:
        h_ref, a_ref, wo_ref, g_ref, wrt_ref, hmid_ref, hf_ref, route_ref, cnt_ref = refs
        hmid = h_ref[...] + _dot(a_ref[...], wo_ref[...])
        hmid_ref[...] = hmid
    else:
        h_ref, g_ref, wrt_ref, hf_ref, route_ref, cnt_ref = refs
        hmid = h_ref[...]
    hf = _rms(hmid, g_ref[...])
    hi = hf.astype(BF16)
    hf_ref[...] = hi
    lo = (hf - hi.astype(F32)).astype(BF16)
    w = wrt_ref[...]
    w_hi = w.astype(BF16)
    w_split = jnp.concatenate([w_hi, (w - w_hi.astype(F32)).astype(BF16)], axis=0)
    part_hi = _dot_nt(w_split, hi)
    logits = part_hi[:N_EXPERTS] + part_hi[N_EXPERTS:] + _dot_nt(w_split, lo)[:N_EXPERTS]
    eidx = lax.broadcasted_iota(jnp.int32, logits.shape, 0)
    m1 = jnp.max(logits, axis=0, keepdims=True)
    i1 = jnp.min(jnp.where(logits == m1, eidx, N_EXPERTS), axis=0, keepdims=True)
    rest = jnp.where(eidx == i1, -jnp.inf, logits)
    m2 = jnp.max(rest, axis=0, keepdims=True)
    i2 = jnp.min(jnp.where(rest == m2, eidx, N_EXPERTS), axis=0, keepdims=True)
    ex = jnp.exp(m2 - m1)
    inv = 1.0 / (1.0 + ex)
    sel = jnp.where(eidx == i1, 1.0, 0.0) + jnp.where(eidx == i2, 1.0, 0.0)
    cnt_ref[0] = jnp.broadcast_to(jnp.sum(sel, axis=1, keepdims=True), cnt_ref.shape[1:])
    route_ref[0] = jnp.where(eidx == 0, i1.astype(F32), jnp.where(eidx == 1, i2.astype(F32),
                             jnp.where(eidx == 2, inv, jnp.where(eidx == 3, ex * inv, 0.0))))


def _router(h2d, attn2d, w_o, g, w_router_t):
    m, d = h2d.shape
    nt = m // TM
    rows = pl.BlockSpec((TM, d), lambda t: (t, 0))
    attn_args, attn_specs, hmid_specs, hmid_shapes = (), [], [], []
    if attn2d is not None:
        attn_args = (attn2d, w_o)
        attn_specs = [pl.BlockSpec((TM, attn2d.shape[1]), lambda t: (t, 0)), pl.BlockSpec(w_o.shape, lambda t: (0, 0))]
        hmid_specs, hmid_shapes = [rows], [jax.ShapeDtypeStruct((m, d), F32)]
    outs = pl.pallas_call(
        functools.partial(_router_kernel, with_attn=attn2d is not None),
        grid=(nt,),
        in_specs=[rows] + attn_specs + [pl.BlockSpec((1, d), lambda t: (0, 0)),
                                        pl.BlockSpec((N_EXPERTS, d), lambda t: (0, 0))],
        out_specs=hmid_specs + [rows,
                                pl.BlockSpec((1, N_EXPERTS, TM), lambda t: (t, 0, 0)),
                                pl.BlockSpec((1, N_EXPERTS, LANES), lambda t: (t, 0, 0))],
        out_shape=hmid_shapes + [jax.ShapeDtypeStruct((m, d), BF16),
                                 jax.ShapeDtypeStruct((nt, N_EXPERTS, TM), F32),
                                 jax.ShapeDtypeStruct((nt, N_EXPERTS, LANES), F32)],
        compiler_params=_cparams(1),
        name="moe_router",
    )(h2d, *attn_args, g, w_router_t)
    return tuple(outs) if attn2d is not None else (h2d, *outs)


def _segment_copies(t, seg_ref, len_ref, base_ref, make_copy, start):
    for e in range(N_EXPERTS):
        n = len_ref[t * N_EXPERTS + e]
        local0 = seg_ref[t * N_EXPERTS + e]
        global0 = base_ref[t * N_EXPERTS + e]
        off = 0
        for p in SEG_SIZES:
            take = (n & p) != 0

            @pl.when(take)
            def _(off=off, p=p):
                cp = make_copy(pl.multiple_of(local0 + off, SEG_ALIGN), pl.multiple_of(global0 + off, SEG_ALIGN), p)
                if start:
                    cp.start()
                else:
                    cp.wait()

            off = off + jnp.where(take, p, 0)


def _zero_unused_rows(gap_ref, na_ref, xs_ref, zero_ref, zsem):
    zero_ref[...] = jnp.zeros_like(zero_ref)
    n_tiles = xs_ref.shape[0] // TM

    def gap_copies(start):
        for e in range(N_EXPERTS):
            n = gap_ref[N_EXPERTS + e]
            off = 0
            for p in SEG_SIZES[1:]:
                take = (n & p) != 0

                @pl.when(take)
                def _(off=off, p=p, e=e):
                    dst = pl.multiple_of(gap_ref[e] + off, SEG_ALIGN)
                    cp = pltpu.make_async_copy(zero_ref.at[pl.ds(0, p)], xs_ref.at[pl.ds(dst, p)], zsem)
                    cp.start() if start else cp.wait()

                off = off + jnp.where(take, p, 0)

    def tile_copy(i, carry, start):
        dst = pl.multiple_of(i * TM, TM)
        cp = pltpu.make_async_copy(zero_ref, xs_ref.at[pl.ds(dst, TM)], zsem)
        cp.start() if start else cp.wait()
        return carry

    gap_copies(True)
    lax.fori_loop(na_ref[0], n_tiles, functools.partial(tile_copy, start=True), 0)
    gap_copies(False)
    lax.fori_loop(na_ref[0], n_tiles, functools.partial(tile_copy, start=False), 0)


def _dispatch_kernel(seg_ref, len_ref, base_ref, gap_ref, na_ref, hf_ref, route_ref, tri_ref, xs_ref, slot_ref,
                     stage_ref, zero_ref, sem, zsem):
    t = pl.program_id(0)
    route = route_ref[0]
    eidx = lax.broadcasted_iota(jnp.int32, route.shape, 0)
    sel1 = eidx == route[0:1, :].astype(jnp.int32)
    sel2 = eidx == route[1:2, :].astype(jnp.int32)
    sel = jnp.where(sel1, 1.0, 0.0) + jnp.where(sel2, 1.0, 0.0)
    sel16 = jnp.concatenate([sel, jnp.zeros_like(sel)], axis=0).astype(BF16)
    rank = _dot(sel16, tri_ref[...])[:N_EXPERTS]
    seg = jnp.zeros(route.shape, jnp.int32)
    for e in range(N_EXPERTS):
        seg = jnp.where(eidx == e, seg_ref[t * N_EXPERTS + e], seg)
    slot = seg.astype(F32) + rank
    slot1 = jnp.sum(jnp.where(sel1, slot, 0.0), axis=0, keepdims=True)
    slot2 = jnp.sum(jnp.where(sel2, slot, 0.0), axis=0, keepdims=True)
    slot_ref[0] = jnp.where(eidx == 0, slot1, jnp.where(eidx == 1, slot2, 0.0))
    r = lax.broadcasted_iota(jnp.int32, (TILE_ROWS, TM), 0)
    perm = jnp.where(r == slot1.astype(jnp.int32), 1.0, jnp.where(r == slot2.astype(jnp.int32), 1.0, 0.0))
    buf = t % 2
    stage_ref[buf] = _dot(perm.astype(BF16), hf_ref[...]).astype(BF16)

    def store(tile, b, start):
        def copy(local, glob, p):
            return pltpu.make_async_copy(stage_ref.at[b, pl.ds(local, p)], xs_ref.at[pl.ds(glob, p)], sem.at[b])

        _segment_copies(tile, seg_ref, len_ref, base_ref, copy, start=start)

    store(t, buf, start=True)

    @pl.when(t > 0)
    def _():
        store(t - 1, 1 - buf, start=False)

    @pl.when(t == pl.num_programs(0) - 1)
    def _():
        store(t, buf, start=False)
        _zero_unused_rows(gap_ref, na_ref, xs_ref, zero_ref, zsem)


def _dispatch(seg, seg_len, base, gaps, n_active, hf, route, tri, n_tiles_max):
    m, d = hf.shape
    nt = m // TM
    grid_spec = pltpu.PrefetchScalarGridSpec(
        num_scalar_prefetch=5,
        grid=(nt,),
        in_specs=[pl.BlockSpec((TM, d), lambda t, *_: (t, 0)),
                  pl.BlockSpec((1, N_EXPERTS, TM), lambda t, *_: (t, 0, 0)),
                  pl.BlockSpec((TM, TM), lambda t, *_: (0, 0))],
        out_specs=[pl.BlockSpec(memory_space=pl.ANY),
                   pl.BlockSpec((1, N_EXPERTS, TM), lambda t, *_: (t, 0, 0))],
        scratch_shapes=[pltpu.VMEM((2, TILE_ROWS, d), BF16), pltpu.VMEM((TM, d), BF16),
                        pltpu.SemaphoreType.DMA((2,)), pltpu.SemaphoreType.DMA(())],
    )
    return pl.pallas_call(
        _dispatch_kernel,
        grid_spec=grid_spec,
        out_shape=[jax.ShapeDtypeStruct((n_tiles_max * TM, d), BF16),
                   jax.ShapeDtypeStruct((nt, N_EXPERTS, TM), F32)],
        compiler_params=_cparams(1),
        name="moe_dispatch",
    )(seg, seg_len, base, gaps, n_active, hf, route, tri)


def _experts_kernel(te_ref, na_ref, x_ref, w1_ref, w3_ref, w2_ref, o_ref, acc_ref, *, nf):
    del te_ref
    i = pl.program_id(0)
    f = pl.program_id(1)

    active = i < na_ref[0]
    last = nf - 1

    def partial_out():
        x = x_ref[...]
        act = jax.nn.silu(_dot(x, w1_ref[0])) * _dot(x, w3_ref[0])
        return _dot(act.astype(BF16), w2_ref[0])

    @pl.when(jnp.logical_and(active, f == 0))
    def _():
        acc_ref[...] = partial_out()

    if nf > 2:
        @pl.when(jnp.logical_and(active, jnp.logical_and(f > 0, f < last)))
        def _():
            acc_ref[...] += partial_out()

    @pl.when(jnp.logical_and(active, f == last))
    def _():
        o_ref[...] = (acc_ref[...] + partial_out()).astype(BF16)

    @pl.when(jnp.logical_and(i >= na_ref[0], f == last))
    def _():
        o_ref[...] = jnp.zeros_like(o_ref)


def _experts(tile_expert, n_active, x_sorted, w1, w3, w2, layer, tf):
    rows, d = x_sorted.shape
    d_ff = w1.shape[3]
    n_tiles, nf = rows // TM, d_ff // tf
    row_blk = lambda i, f, te, na: (jnp.minimum(i, na[0] - 1), 0)
    chunk = lambda i, f, na: jnp.where(i < na[0], f, nf - 1)
    grid_spec = pltpu.PrefetchScalarGridSpec(
        num_scalar_prefetch=2,
        grid=(n_tiles, nf),
        in_specs=[pl.BlockSpec((TM, d), row_blk),
                  pl.BlockSpec((None, 1, d, tf), lambda i, f, te, na: (layer, te[i], 0, chunk(i, f, na))),
                  pl.BlockSpec((None, 1, d, tf), lambda i, f, te, na: (layer, te[i], 0, chunk(i, f, na))),
                  pl.BlockSpec((None, 1, tf, d), lambda i, f, te, na: (layer, te[i], chunk(i, f, na), 0))],
        out_specs=pl.BlockSpec((TM, d), lambda i, f, te, na: (i, 0)),
        scratch_shapes=[pltpu.VMEM((TM, d), F32)],
    )
    assert nf >= 2
    return pl.pallas_call(
        functools.partial(_experts_kernel, nf=nf),
        grid_spec=grid_spec,
        out_shape=jax.ShapeDtypeStruct((rows, d), BF16),
        compiler_params=_cparams(2),
        name="moe_experts",
    )(tile_expert, n_active, x_sorted, w1, w3, w2)


def _combine_kernel(seg_ref, len_ref, base_ref, h_ref, slot_ref, gate_ref, y_ref, o_ref, ybuf_ref, sem):
    t = pl.program_id(0)
    buf = t % 2

    def fetch(tile, b, start):
        def copy(local, glob, p):
            return pltpu.make_async_copy(y_ref.at[pl.ds(glob, p)], ybuf_ref.at[b, pl.ds(local, p)], sem.at[b])

        if start:
            ybuf_ref[b, 2 * TM:, :] = jnp.zeros((TILE_ROWS - 2 * TM, ybuf_ref.shape[2]), BF16)
        _segment_copies(tile, seg_ref, len_ref, base_ref, copy, start=start)

    @pl.when(t == 0)
    def _():
        fetch(t, buf, start=True)

    @pl.when(t + 1 < pl.num_programs(0))
    def _():
        fetch(t + 1, 1 - buf, start=True)

    fetch(t, buf, start=False)

    y = ybuf_ref[buf]
    lane = lax.broadcasted_iota(jnp.int32, (TM, TILE_ROWS), 1)
    out = h_ref[...]
    for k in range(2):
        onehot = jnp.where(lane == slot_ref[:, k:k + 1], 1.0, 0.0).astype(BF16)
        out = out + gate_ref[:, k:k + 1] * _dot(onehot, y)
    o_ref[...] = out


def _combine(seg, seg_len, base, h2d, slots, gates, y_sorted):
    m, d = h2d.shape
    nt = m // TM
    grid_spec = pltpu.PrefetchScalarGridSpec(
        num_scalar_prefetch=3,
        grid=(nt,),
        in_specs=[pl.BlockSpec((TM, d), lambda t, *_: (t, 0)),
                  pl.BlockSpec((TM, 2), lambda t, *_: (t, 0)),
                  pl.BlockSpec((TM, 2), lambda t, *_: (t, 0)),
                  pl.BlockSpec(memory_space=pl.ANY)],
        out_specs=pl.BlockSpec((TM, d), lambda t, *_: (t, 0)),
        scratch_shapes=[pltpu.VMEM((2, TILE_ROWS, d), BF16), pltpu.SemaphoreType.DMA((2,))],
    )
    return pl.pallas_call(
        _combine_kernel,
        grid_spec=grid_spec,
        out_shape=jax.ShapeDtypeStruct((m, d), F32),
        compiler_params=_cparams(1),
        name="moe_combine",
    )(seg, seg_len, base, h2d, slots, gates, y_sorted)


def _moe_tables(counts, n_tiles_max):
    pad = (counts + SEG_ALIGN - 1) // SEG_ALIGN * SEG_ALIGN
    seg = jnp.cumsum(pad, axis=1) - pad
    used = jnp.sum(pad, axis=0)
    region = (used + TM - 1) // TM * TM
    region_start = jnp.cumsum(region) - region
    base = region_start[None, :] + jnp.cumsum(pad, axis=0) - pad
    gaps = jnp.concatenate([region_start + used, region - used])
    tiles_end = jnp.cumsum(region // TM)
    n_active = tiles_end[-1]
    tile = jnp.minimum(jnp.arange(n_tiles_max), n_active - 1)
    tile_expert = jnp.sum(tile[:, None] >= tiles_end[None, :], axis=1)
    flat = lambda a: a.reshape(-1).astype(jnp.int32)
    return flat(seg), flat(pad), flat(base), flat(gaps), flat(tile_expert), flat(n_active)


def _moe_layer(h, attn, w_o, g, w_router, w1, w3, w2, layer, tf):
    bsz, t, d = h.shape
    m = bsz * t
    nt = m // TM
    n_tiles_max = -(-(nt * (TILE_ROWS - SEG_ALIGN) + N_EXPERTS * (TM - 1)) // TM)
    attn2d = None if attn is None else attn.reshape(m, attn.shape[2])
    h2d, hf, route, counts = _router(h.reshape(m, d), attn2d, w_o, g, w_router.T)
    seg, seg_len, base, gaps, tile_expert, n_active = _moe_tables(counts[:, :, 0].astype(jnp.int32), n_tiles_max)
    tri = (jnp.arange(TM)[:, None] < jnp.arange(TM)[None, :]).astype(BF16)
    x_sorted, slots = _dispatch(seg, seg_len, base, gaps, n_active, hf, route, tri, n_tiles_max)
    y_sorted = _experts(tile_expert, n_active, x_sorted, w1, w3, w2, layer, tf)
    to_cols = lambda a: a.transpose(0, 2, 1).reshape(m, 2)
    out = _combine(seg, seg_len, base, h2d, to_cols(slots[:, :2, :]).astype(jnp.int32),
                   to_cols(route[:, 2:4, :]), y_sorted)
    return out.reshape(bsz, t, d)


def _kv_kernel(h_ref, g_ref, wd_ref, gl_ref, cs_ref, wk_ref, wv_ref, k_ref, v_ref):
    hs = _rms(h_ref[0], g_ref[...]).astype(BF16)
    ckv = _dot(hs, wd_ref[...])
    c_lat = _rms(ckv[:, :KV_LORA], gl_ref[...])
    pe = ckv[:, KV_LORA:KV_LORA + QK_ROPE]
    pe_rot = ckv[:, KV_LORA + QK_ROPE:]
    k_pe = pe * cs_ref[:, :QK_ROPE] + pe_rot * cs_ref[:, QK_ROPE:]
    lat = jnp.concatenate([c_lat, k_pe], axis=1).astype(BF16)
    k_ref[0] = _dot(lat, wk_ref[...]).astype(BF16)
    v_ref[0] = _dot(lat[:, :KV_LORA], wv_ref[...]).astype(BF16)


def _kv_proj(h, g_src, w_down_aug, g_latent, cs_k, w_k, w_v):
    bsz, t, d = h.shape
    nt = t // TT
    full = lambda shape: pl.BlockSpec(shape, lambda b, i: (0,) * len(shape))
    return pl.pallas_call(
        _kv_kernel,
        grid=(bsz, nt),
        in_specs=[pl.BlockSpec((1, TT, d), lambda b, i: (b, i, 0)), full((1, d)), full(w_down_aug.shape),
                  full((1, KV_LORA)), pl.BlockSpec((TT, 2 * QK_ROPE), lambda b, i: (i, 0)),
                  full(w_k.shape), full(w_v.shape)],
        out_specs=[pl.BlockSpec((1, TT, N_HEADS * HEAD_PAD), lambda b, i: (b, i, 0)),
                   pl.BlockSpec((1, TT, N_HEADS * V_HEAD), lambda b, i: (b, i, 0))],
        out_shape=[jax.ShapeDtypeStruct((bsz, t, N_HEADS * HEAD_PAD), BF16),
                   jax.ShapeDtypeStruct((bsz, t, N_HEADS * V_HEAD), BF16)],
        compiler_params=_cparams(2),
        name="kv_proj",
    )(h, g_src, w_down_aug, g_latent, cs_k, w_k, w_v)


def _q_kernel(h_ref, g_ref, wdq_ref, gq_ref, wuq_ref, ct_ref, st_ref, q_ref):
    hn = _rms(h_ref[0], g_ref[...]).astype(BF16)
    c_q = _rms(_dot(hn, wdq_ref[...]), gq_ref[...]).astype(BF16)
    q = _dot(c_q, wuq_ref[...])
    width = q.shape[1]
    ct = jnp.tile(ct_ref[...], (1, N_HEADS))
    st = jnp.tile(st_ref[...], (1, N_HEADS))
    q_ref[0] = (q * ct + pltpu.roll(q, width - QK_ROPE, axis=1) * st).astype(BF16)


def _q_proj(h, g, w_dq, g_q, w_uq_aug, ctab, stab):
    bsz, t, d = h.shape
    nt = t // TT
    full = lambda shape: pl.BlockSpec(shape, lambda b, i: (0,) * len(shape))
    tab = pl.BlockSpec((TT, HEAD_PAD), lambda b, i: (i, 0))
    return pl.pallas_call(
        _q_kernel,
        grid=(bsz, nt),
        in_specs=[pl.BlockSpec((1, TT, d), lambda b, i: (b, i, 0)), full((1, d)), full(w_dq.shape),
                  full((1, Q_LORA)), full(w_uq_aug.shape), tab, tab],
        out_specs=pl.BlockSpec((1, TT, N_HEADS * HEAD_PAD), lambda b, i: (b, i, 0)),
        out_shape=jax.ShapeDtypeStruct((bsz, t, N_HEADS * HEAD_PAD), BF16),
        compiler_params=_cparams(2),
        name="q_proj",
    )(h, g, w_dq, g_q, w_uq_aug, ctab, stab)


def _attn_kernel(q_ref, k_ref, v_ref, o_ref, s_ref, vt_ref):
    n_qblk = (q_ref.shape[1] - N_META) // Q_BLOCK
    heads = (0, 1)
    qk_sl = [slice(hh * HEAD_PAD, (hh + 1) * HEAD_PAD) for hh in heads]
    pair_v = 2 * V_HEAD

    def v_ext(vv):
        n = vv.shape[0]
        head0 = lax.broadcasted_iota(jnp.int32, (n, pair_v), 1) < V_HEAD
        sel0 = jnp.where(head0, 1.0, 0.0).astype(BF16)
        sel1 = jnp.where(head0, 0.0, 1.0).astype(BF16)
        r = lax.broadcasted_iota(jnp.int32, (2 * n, pair_v), 0)
        c = lax.broadcasted_iota(jnp.int32, (2 * n, pair_v), 1)
        ones = jnp.where(c == jnp.where(r < n, 0, 1), 1.0, 0.0).astype(BF16)
        return jnp.concatenate([jnp.concatenate([vv * sel0, vv * sel1], axis=0), ones], axis=1)

    def normalise(acc):
        head0 = lax.broadcasted_iota(jnp.int32, (acc.shape[0], pair_v), 1) < V_HEAD
        inv = jnp.where(head0, 1.0 / acc[:, pair_v:pair_v + 1], 1.0 / acc[:, pair_v + 1:pair_v + 2])
        return (acc[:, :pair_v] * inv).astype(BF16)

    k_meta = [k_ref[0, 0:N_META, qk_sl[hh]] for hh in heads]
    v_meta = v_ext(v_ref[0, 0:N_META, :])

    ps = []
    for hh in heads:
        s = _dot_nt(q_ref[0, 0:N_META, qk_sl[hh]], k_meta[hh])
        ps.append(jnp.exp2(s - jnp.max(s, axis=1, keepdims=True)))
    o_ref[0, 0:N_META, :] = normalise(_dot(jnp.concatenate(ps, axis=1).astype(BF16), v_meta))

    kc = lax.broadcasted_iota(jnp.int32, (Q_BLOCK, Q_BLOCK), 0) // CHUNK
    qc = lax.broadcasted_iota(jnp.int32, (Q_BLOCK, Q_BLOCK), 1) // CHUNK
    diag_mask = kc <= qc
    eye = jnp.where(lax.broadcasted_iota(jnp.int32, (pair_v, pair_v), 0)
                    == lax.broadcasted_iota(jnp.int32, (pair_v, pair_v), 1), 1.0, 0.0).astype(BF16)

    def vt_ext(vv):
        vt = _dot_nt(eye, vv).astype(BF16)
        ones = jnp.ones((SEG_ALIGN, vv.shape[0]), BF16)
        return [jnp.concatenate([vt[hh * V_HEAD:(hh + 1) * V_HEAD], ones], axis=0) for hh in heads]

    vt_meta = vt_ext(v_ref[0, 0:N_META, :])
    n_kblk = n_qblk
    for j in range(n_kblk):
        c0 = N_META + j * Q_BLOCK
        blk = vt_ext(v_ref[0, c0:c0 + Q_BLOCK, :])
        for hh in heads:
            vt_ref[j, hh] = blk[hh]

    def fold(s):
        return jnp.max(s.reshape(s.shape[0] // SUBLANES, SUBLANES, s.shape[1]), axis=0)

    for i in range(n_qblk):
        r0 = N_META + i * Q_BLOCK
        qs = [q_ref[0, r0:r0 + Q_BLOCK, qk_sl[hh]] for hh in heads]
        s_meta = [_dot_nt(k_meta[hh], qs[hh]) for hh in heads]
        mf = [None, None]
        for j in range(i + 1):
            c0 = N_META + j * Q_BLOCK
            for hh in heads:
                s = _dot_nt(k_ref[0, c0:c0 + Q_BLOCK, qk_sl[hh]], qs[hh])
                if j == i:
                    s = jnp.where(diag_mask, s, MASK_VALUE)
                s_ref[j, hh] = s
                mf[hh] = fold(s) if mf[hh] is None else jnp.maximum(mf[hh], fold(s))
        outs = []
        for hh in heads:
            m = jnp.maximum(jnp.max(mf[hh], axis=0, keepdims=True), jnp.max(s_meta[hh], axis=0, keepdims=True))
            acc = _dot(vt_meta[hh], jnp.exp2(s_meta[hh] - m).astype(BF16))
            for j in range(i + 1):
                acc = acc + _dot(vt_ref[j, hh], jnp.exp2(s_ref[j, hh] - m).astype(BF16))
            outs.append(acc[:V_HEAD] * (1.0 / acc[V_HEAD:V_HEAD + 1]))
        o_ref[0, r0:r0 + Q_BLOCK, :] = jnp.concatenate(outs, axis=0).T.astype(BF16)


def _attention(q, k, v):
    bsz, t, _ = q.shape
    n_blk = (t - N_META) // Q_BLOCK
    qk_spec = pl.BlockSpec((1, t, 2 * HEAD_PAD), lambda b, p: (b, 0, p))
    v_spec = pl.BlockSpec((1, t, 2 * V_HEAD), lambda b, p: (b, 0, p))
    return pl.pallas_call(
        _attn_kernel,
        grid=(bsz, N_HEADS // 2),
        in_specs=[qk_spec, qk_spec, v_spec],
        out_specs=v_spec,
        out_shape=jax.ShapeDtypeStruct((bsz, t, N_HEADS * V_HEAD), BF16),
        scratch_shapes=[pltpu.VMEM((n_blk, 2, Q_BLOCK, Q_BLOCK), F32),
                        pltpu.VMEM((n_blk, 2, V_HEAD + SEG_ALIGN, Q_BLOCK), BF16)],
        compiler_params=_cparams(2),
        name="attention",
    )(q, k, v)


def _final_kernel(h_ref, g_ref, o_ref):
    o_ref[0] = _rms(h_ref[0, N_META:, :], g_ref[...])


def _final_norm(h, g):
    bsz, t, d = h.shape
    return pl.pallas_call(
        _final_kernel,
        grid=(bsz,),
        in_specs=[pl.BlockSpec((1, t, d), lambda b: (b, 0, 0)), pl.BlockSpec((1, d), lambda b: (0, 0))],
        out_specs=pl.BlockSpec((1, t - N_META, d), lambda b: (b, 0, 0)),
        out_shape=jax.ShapeDtypeStruct((bsz, t - N_META, d), F32),
        compiler_params=_cparams(1),
        name="final_norm",
    )(h, g)


def _rot_cols(w):
    half = w.shape[1] // 2
    return jnp.concatenate([-w[:, half:], w[:, :half]], axis=1)


def _gate_pairs(w_a, w_i):
    def bd(w):
        z = jnp.zeros((RNN_BLOCK, RNN_BLOCK), w.dtype)
        return jnp.stack([jnp.block([[w[2 * p], z], [z, w[2 * p + 1]]]) for p in range(N_RNN_BLOCKS // 2)])
    return jnp.concatenate([bd(w_a), bd(w_i)], axis=2).astype(BF16)


def _q_up_aug(w_uq):
    w = w_uq.reshape(Q_LORA, N_HEADS, QK_NOPE + QK_ROPE)
    pe = w[:, :, QK_NOPE:]
    rot = jnp.concatenate([-pe[:, :, QK_ROPE // 2:], pe[:, :, :QK_ROPE // 2]], axis=2)
    return jnp.concatenate([w, rot], axis=2).reshape(Q_LORA, N_HEADS * HEAD_PAD).astype(BF16)


def _kv_up_aug(w_up):
    w = w_up.reshape(KV_LORA, N_HEADS, QK_NOPE + V_HEAD)
    pad = HEAD_PAD - QK_NOPE
    w_k_top = jnp.concatenate([w[:, :, :QK_NOPE], jnp.zeros((KV_LORA, N_HEADS, pad), w.dtype)], axis=2)
    eye = jnp.concatenate([jnp.zeros((QK_ROPE, QK_NOPE), w.dtype), jnp.eye(QK_ROPE, dtype=w.dtype),
                           jnp.zeros((QK_ROPE, pad - QK_ROPE), w.dtype)], axis=1)
    w_k_bot = jnp.broadcast_to(eye[:, None, :], (QK_ROPE, N_HEADS, HEAD_PAD))
    w_k = jnp.concatenate([w_k_top, w_k_bot], axis=0).reshape(KV_LORA + QK_ROPE, N_HEADS * HEAD_PAD)
    w_v = w[:, :, QK_NOPE:].reshape(KV_LORA, N_HEADS * V_HEAD)
    return w_k.astype(BF16), w_v.astype(BF16)


def _rope_tables(t):
    inv_freq = ROPE_THETA ** (-jnp.arange(0, QK_ROPE, 2, dtype=F32) / QK_ROPE)
    ang = jnp.arange(t, dtype=F32)[:, None] * inv_freq[None, :]
    cos = jnp.tile(jnp.cos(ang), (1, 2))
    sin = jnp.tile(jnp.sin(ang), (1, 2))
    cs_k = jnp.concatenate([cos, sin], axis=1)
    ones = jnp.ones((t, QK_NOPE), F32)
    zeros = jnp.zeros((t, HEAD_PAD - QK_NOPE - QK_ROPE), F32)
    ctab = SCORE_SCALE * jnp.concatenate([ones, cos, zeros], axis=1)
    stab = SCORE_SCALE * jnp.concatenate([0.0 * ones, sin, zeros], axis=1)
    return cs_k, ctab, stab


def kernel(x, meta_tokens, norm_mix, norm_ffn, norm_final, rnn_w_in, rnn_conv_w, rnn_conv_b, rnn_w_a, rnn_b_a, rnn_w_i, rnn_b_i, rnn_lambda, rnn_w_out, kv_norm_src, kv_w_down, kv_latent_norm, kv_w_up, q_w_down, q_latent_norm, q_w_up, attn_w_out, ffn_w1, ffn_w3, ffn_w2, moe_router, moe_w1, moe_w3, moe_w2):
    bsz = x.shape[0]
    meta = jnp.broadcast_to(meta_tokens.astype(x.dtype)[None], (bsz, N_META, x.shape[-1]))
    h = jnp.concatenate([meta, x], axis=1)
    t = h.shape[1]
    assert t % TT == 0 and (t - N_META) % Q_BLOCK == 0
    row = lambda v: v.reshape(1, -1)
    cs_k, ctab, stab = _rope_tables(t)

    ffn_w = [w.astype(BF16) for w in (ffn_w1, ffn_w3, ffn_w2)]
    moe_w = [w.astype(BF16) for w in (moe_w1, moe_w3, moe_w2)]
    k = v = attn = w_o = None
    for layer in range(DEPTH):
        if layer < N_A_LAYERS:
            a = layer
            h = _rnn_layer(h, row(norm_mix[layer]), rnn_w_in[a].astype(BF16), rnn_conv_w[a], row(rnn_conv_b[a]),
                           _gate_pairs(rnn_w_a[a], rnn_w_i[a]), row(rnn_b_a[a]), row(rnn_b_i[a]),
                           row(rnn_lambda[a]), rnn_w_out[a].astype(BF16))
        else:
            b = layer - N_A_LAYERS
            if b == 0:
                w_down_aug = jnp.concatenate([kv_w_down, _rot_cols(kv_w_down[:, KV_LORA:])], axis=1).astype(BF16)
                w_k, w_v = _kv_up_aug(kv_w_up)
                k, v = _kv_proj(h, row(kv_norm_src), w_down_aug, row(kv_latent_norm), cs_k, w_k, w_v)
            q = _q_proj(h, row(norm_mix[layer]), q_w_down[b].astype(BF16), row(q_latent_norm[b]),
                        _q_up_aug(q_w_up[b]), ctab, stab)
            attn, w_o = _attention(q, k, v), attn_w_out[b].astype(BF16)
        if layer % 2 == 0:
            h = _ffn_layer(h, attn, w_o, row(norm_ffn[layer]), *ffn_w, e=layer // 2)
        else:
            h = _moe_layer(h, attn, w_o, row(norm_ffn[layer]), moe_router[layer // 2], *moe_w,
                           layer=layer // 2, tf=1792)
        attn = None
    return _final_norm(h, row(norm_final))
```

```python
import functools
import math

import jax
import jax.numpy as jnp
from jax import lax
from jax.experimental import pallas as pl
from jax.experimental.pallas import tpu as pltpu

D_MODEL = 1024
N_META = 16
CHUNK = 64
NORM_EPS = 1e-6
DEPTH = 4
N_A_LAYERS = DEPTH // 2

D_RNN = D_MODEL
N_RNN_BLOCKS = 8
RNN_BLOCK = D_RNN // N_RNN_BLOCKS
CONV_WIDTH = 4
LRU_C = 8.0

N_HEADS = 16
QK_NOPE = 64
QK_ROPE = 32
V_HEAD = 64
Q_LORA = 384
KV_LORA = 256
ROPE_THETA = 10000.0
ATTN_SCALE = 1.0 / math.sqrt(QK_NOPE + QK_ROPE)
MASK_VALUE = -1e30
SCORE_SCALE = ATTN_SCALE * math.log2(math.e)

N_EXPERTS = 8

LANES = 128
SUBLANES = 8
VMEM_LIMIT = 56 * 1024 * 1024

HEAD_PAD = 128
Q_BLOCK = 512
TT = 688
FF_CHUNK = 1024
SCAN_GROUPS = 2
TM = 512
SEG_ALIGN = 16
SEG_SIZES = (512, 256, 128, 64, 32, 16)
TILE_ROWS = 2 * TM + N_EXPERTS * SEG_ALIGN
F32 = jnp.float32
BF16 = jnp.bfloat16


def _cparams(n_axes):
    return pltpu.CompilerParams(dimension_semantics=("arbitrary",) * n_axes,
                                vmem_limit_bytes=VMEM_LIMIT)


def _rms(x, g):
    return x * lax.rsqrt(jnp.mean(x * x, axis=-1, keepdims=True) + NORM_EPS) * g


def _sigmoid(x):
    return 0.5 * jnp.tanh(0.5 * x) + 0.5


def _dot(a, b):
    return jnp.dot(a, b, preferred_element_type=F32)


def _dot_nt(a, b):
    return lax.dot_general(a, b, (((1,), (1,)), ((), ())), preferred_element_type=F32)


def _rnn_kernel(h_ref, g_ref, win_ref, cw_ref, cb_ref, wg_ref, ba_ref, bi_ref, lam_ref, wout_ref,
                o_ref, xpad_ref, yg_ref, a_ref, u_ref, carry_ref):
    t = pl.program_id(1)
    tt = h_ref.shape[1]

    @pl.when(t == 0)
    def _():
        xpad_ref[0:SUBLANES, :] = jnp.zeros((SUBLANES, D_RNN), F32)
        carry_ref[...] = jnp.zeros_like(carry_ref)

    x = h_ref[0]
    hn = _rms(x, g_ref[...]).astype(BF16)
    yg_ref[...] = jax.nn.gelu(_dot(hn, win_ref[:, :D_RNN]), approximate=True)
    xpad_ref[SUBLANES:, :] = _dot(hn, win_ref[:, D_RNN:])

    xc = cb_ref[...] + cw_ref[CONV_WIDTH - 1:CONV_WIDTH, :] * xpad_ref[SUBLANES:, :]
    for j in range(CONV_WIDTH - 1):
        off = SUBLANES - (CONV_WIDTH - 1) + j
        xc = xc + cw_ref[j:j + 1, :] * xpad_ref[off:off + tt, :]
    xpad_ref[0:SUBLANES, :] = xpad_ref[tt:tt + SUBLANES, :]

    xcb = xc.astype(BF16)
    log_coef = -LRU_C * jax.nn.softplus(-lam_ref[...])
    pair = 2 * RNN_BLOCK
    for p in range(N_RNN_BLOCKS // 2):
        sl = slice(p * pair, (p + 1) * pair)
        gates = _dot(xcb[:, sl], wg_ref[p])
        gate_r = _sigmoid(gates[:, :pair] + ba_ref[:, sl])
        gate_i = _sigmoid(gates[:, pair:] + bi_ref[:, sl])
        log_a = log_coef[:, sl] * gate_r
        a_ref[:, sl] = jnp.exp(log_a)
        th = jnp.tanh(log_a)
        u_ref[:, sl] = jnp.sqrt(-2.0 * th / (1.0 - th)) * (gate_i * xc[:, sl])

    row = lax.broadcasted_iota(jnp.int32, (SUBLANES, D_RNN), 0)

    def groups(i, carry):
        local = []
        for g in range(SCAN_GROUPS):
            r0 = pl.multiple_of((i * SCAN_GROUPS + g) * SUBLANES, SUBLANES)
            a = a_ref[pl.ds(r0, SUBLANES), :]
            u = u_ref[pl.ds(r0, SUBLANES), :]
            for s in (1, 2, 4):
                keep = row >= s
                u = jnp.where(keep, a * pltpu.roll(u, s, axis=0) + u, u)
                a = jnp.where(keep, a * pltpu.roll(a, s, axis=0), a)
            local.append((r0, a, u))
        for r0, a, u in local:
            hs = a * carry + u
            u_ref[pl.ds(r0, SUBLANES), :] = hs
            carry = hs[SUBLANES - 1:SUBLANES, :]
        return carry

    carry_ref[...] = lax.fori_loop(0, tt // (SCAN_GROUPS * SUBLANES), groups, carry_ref[...])

    y = (u_ref[...] * yg_ref[...]).astype(BF16)
    o_ref[0] = x + _dot(y, wout_ref[...])


def _rnn_layer(h, g, w_in, conv_w, conv_b, w_gate, b_a, b_i, lam, w_out):
    bsz, t, d = h.shape
    nt = t // TT
    full = lambda shape: pl.BlockSpec(shape, lambda b, i: (0,) * len(shape))
    tile = pl.BlockSpec((1, TT, d), lambda b, i: (b, i, 0))
    return pl.pallas_call(
        _rnn_kernel,
        grid=(bsz, nt),
        in_specs=[tile, full((1, d)), full((d, 2 * D_RNN)), full((CONV_WIDTH, D_RNN)), full((1, D_RNN)),
                  full(w_gate.shape), full((1, D_RNN)), full((1, D_RNN)), full((1, D_RNN)),
                  full((D_RNN, d))],
        out_specs=tile,
        out_shape=jax.ShapeDtypeStruct(h.shape, F32),
        scratch_shapes=[pltpu.VMEM((TT + SUBLANES, D_RNN), F32), pltpu.VMEM((TT, D_RNN), F32),
                        pltpu.VMEM((TT, D_RNN), F32), pltpu.VMEM((TT, D_RNN), F32),
                        pltpu.VMEM((1, D_RNN), F32)],
        compiler_params=_cparams(2),
        name="rnn_mixer",
    )(h, g, w_in, conv_w, conv_b, w_gate, b_a, b_i, lam, w_out)


def _ffn_kernel(*refs, with_attn):
    if with_attn:
        h_ref, a_ref, wo_ref, g_ref, w1_ref, w3_ref, w2_ref, o_ref = refs
    else:
        h_ref, g_ref, w1_ref, w3_ref, w2_ref, o_ref = refs
    out = h_ref[0]
    if with_attn:
        out = out + _dot(a_ref[0], wo_ref[...])
    hf = _rms(out, g_ref[...]).astype(BF16)
    d_ff = w1_ref.shape[1]
    for c0 in range(0, d_ff, FF_CHUNK):
        c1 = min(c0 + FF_CHUNK, d_ff)
        act = jax.nn.silu(_dot(hf, w1_ref[:, c0:c1])) * _dot(hf, w3_ref[:, c0:c1])
        out = out + _dot(act.astype(BF16), w2_ref[c0:c1, :])
    o_ref[0] = out


def _ffn_layer(h, attn, w_o, g, w1, w3, w2, e):
    bsz, t, d = h.shape
    d_ff = w1.shape[2]
    nt = t // TT
    tile = pl.BlockSpec((1, TT, d), lambda b, i: (b, i, 0))
    resident = lambda shape, idx: pl.BlockSpec(shape, lambda b, i: idx, pipeline_mode=pl.Buffered(1))
    attn_args, attn_specs = (), []
    if attn is not None:
        attn_args = (attn, w_o)
        attn_specs = [pl.BlockSpec((1, TT, attn.shape[2]), lambda b, i: (b, i, 0)),
                      resident(w_o.shape, (0, 0))]
    return pl.pallas_call(
        functools.partial(_ffn_kernel, with_attn=attn is not None),
        grid=(bsz, nt),
        in_specs=[tile] + attn_specs +
                 [pl.BlockSpec((1, d), lambda b, i: (0, 0)),
                  resident((None, d, d_ff), (e, 0, 0)),
                  resident((None, d, d_ff), (e, 0, 0)),
                  resident((None, d_ff, d), (e, 0, 0))],
        out_specs=tile,
        out_shape=jax.ShapeDtypeStruct(h.shape, F32),
        compiler_params=_cparams(2),
        name="ffn_dense",
    )(h, *attn_args, g, w1, w3, w2)


def _router_kernel(*refs, with_attn):
    if with_attn:
        h_ref, a_ref, wo_ref, g_ref, wrt_ref, hmid_ref, hf_ref, route_ref, cnt_ref = refs
        hmid = h_ref[...] + _dot(a_ref[...], wo_ref[...])
        hmid_ref[...] = hmid
    else:
        h_ref, g_ref, wrt_ref, hf_ref, route_ref, cnt_ref = refs
        hmid = h_ref[...]
    hf = _rms(hmid, g_ref[...])
    hi = hf.astype(BF16)
    hf_ref[...] = hi
    lo = (hf - hi.astype(F32)).astype(BF16)
    w = wrt_ref[...]
    w_hi = w.astype(BF16)
    w_split = jnp.concatenate([w_hi, (w - w_hi.astype(F32)).astype(BF16)], axis=0)
    part_hi = _dot_nt(w_split, hi)
    logits = part_hi[:N_EXPERTS] + part_hi[N_EXPERTS:] + _dot_nt(w_split, lo)[:N_EXPERTS]
    eidx = lax.broadcasted_iota(jnp.int32, logits.shape, 0)
    m1 = jnp.max(logits, axis=0, keepdims=True)
    i1 = jnp.min(jnp.where(logits == m1, eidx, N_EXPERTS), axis=0, keepdims=True)
    rest = jnp.where(eidx == i1, -jnp.inf, logits)
    m2 = jnp.max(rest, axis=0, keepdims=True)
    i2 = jnp.min(jnp.where(rest == m2, eidx, N_EXPERTS), axis=0, keepdims=True)
    ex = jnp.exp(m2 - m1)
    inv = 1.0 / (1.0 + ex)
    sel = jnp.where(eidx == i1, 1.0, 0.0) + jnp.where(eidx == i2, 1.0, 0.0)
    cnt_ref[0] = jnp.broadcast_to(jnp.sum(sel, axis=1, keepdims=True), cnt_ref.shape[1:])
    route_ref[0] = jnp.where(eidx == 0, i1.astype(F32), jnp.where(eidx == 1, i2.astype(F32),
                             jnp.where(eidx == 2, inv, jnp.where(eidx == 3, ex * inv, 0.0))))


def _router(h2d, attn2d, w_o, g, w_router_t):
    m, d = h2d.shape
    nt = m // TM
    rows = pl.BlockSpec((TM, d), lambda t: (t, 0))
    attn_args, attn_specs, hmid_specs, hmid_shapes = (), [], [], []
    if attn2d is not None:
        attn_args = (attn2d, w_o)
        attn_specs = [pl.BlockSpec((TM, attn2d.shape[1]), lambda t: (t, 0)), pl.BlockSpec(w_o.shape, lambda t: (0, 0))]
        hmid_specs, hmid_shapes = [rows], [jax.ShapeDtypeStruct((m, d), F32)]
    outs = pl.pallas_call(
        functools.partial(_router_kernel, with_attn=attn2d is not None),
        grid=(nt,),
        in_specs=[rows] + attn_specs + [pl.BlockSpec((1, d), lambda t: (0, 0)),
                                        pl.BlockSpec((N_EXPERTS, d), lambda t: (0, 0))],
        out_specs=hmid_specs + [rows,
                                pl.BlockSpec((1, N_EXPERTS, TM), lambda t: (t, 0, 0)),
                                pl.BlockSpec((1, N_EXPERTS, LANES), lambda t: (t, 0, 0))],
        out_shape=hmid_shapes + [jax.ShapeDtypeStruct((m, d), BF16),
                                 jax.ShapeDtypeStruct((nt, N_EXPERTS, TM), F32),
                                 jax.ShapeDtypeStruct((nt, N_EXPERTS, LANES), F32)],
        compiler_params=_cparams(1),
        name="moe_router",
    )(h2d, *attn_args, g, w_router_t)
    return tuple(outs) if attn2d is not None else (h2d, *outs)


def _segment_copies(t, seg_ref, len_ref, base_ref, make_copy, start):
    for e in range(N_EXPERTS):
        n = len_ref[t * N_EXPERTS + e]
        local0 = seg_ref[t * N_EXPERTS + e]
        global0 = base_ref[t * N_EXPERTS + e]
        off = 0
        for p in SEG_SIZES:
            take = (n & p) != 0

            @pl.when(take)
            def _(off=off, p=p):
                cp = make_copy(pl.multiple_of(local0 + off, SEG_ALIGN), pl.multiple_of(global0 + off, SEG_ALIGN), p)
                if start:
                    cp.start()
                else:
                    cp.wait()

            off = off + jnp.where(take, p, 0)


def _zero_unused_rows(gap_ref, na_ref, xs_ref, zero_ref, zsem):
    zero_ref[...] = jnp.zeros_like(zero_ref)
    n_tiles = xs_ref.shape[0] // TM

    def gap_copies(start):
        for e in range(N_EXPERTS):
            n = gap_ref[N_EXPERTS + e]
            off = 0
            for p in SEG_SIZES[1:]:
                take = (n & p) != 0

                @pl.when(take)
                def _(off=off, p=p, e=e):
                    dst = pl.multiple_of(gap_ref[e] + off, SEG_ALIGN)
                    cp = pltpu.make_async_copy(zero_ref.at[pl.ds(0, p)], xs_ref.at[pl.ds(dst, p)], zsem)
                    cp.start() if start else cp.wait()

                off = off + jnp.where(take, p, 0)

    def tile_copy(i, carry, start):
        dst = pl.multiple_of(i * TM, TM)
        cp = pltpu.make_async_copy(zero_ref, xs_ref.at[pl.ds(dst, TM)], zsem)
        cp.start() if start else cp.wait()
        return carry

    gap_copies(True)
    lax.fori_loop(na_ref[0], n_tiles, functools.partial(tile_copy, start=True), 0)
    gap_copies(False)
    lax.fori_loop(na_ref[0], n_tiles, functools.partial(tile_copy, start=False), 0)


def _dispatch_kernel(seg_ref, len_ref, base_ref, gap_ref, na_ref, hf_ref, route_ref, tri_ref, xs_ref, slot_ref,
                     stage_ref, zero_ref, sem, zsem):
    t = pl.program_id(0)
    route = route_ref[0]
    eidx = lax.broadcasted_iota(jnp.int32, route.shape, 0)
    sel1 = eidx == route[0:1, :].astype(jnp.int32)
    sel2 = eidx == route[1:2, :].astype(jnp.int32)
    sel = jnp.where(sel1, 1.0, 0.0) + jnp.where(sel2, 1.0, 0.0)
    sel16 = jnp.concatenate([sel, jnp.zeros_like(sel)], axis=0).astype(BF16)
    rank = _dot(sel16, tri_ref[...])[:N_EXPERTS]
    seg = jnp.zeros(route.shape, jnp.int32)
    for e in range(N_EXPERTS):
        seg = jnp.where(eidx == e, seg_ref[t * N_EXPERTS + e], seg)
    slot = seg.astype(F32) + rank
    slot1 = jnp.sum(jnp.where(sel1, slot, 0.0), axis=0, keepdims=True)
    slot2 = jnp.sum(jnp.where(sel2, slot, 0.0), axis=0, keepdims=True)
    slot_ref[0] = jnp.where(eidx == 0, slot1, jnp.where(eidx == 1, slot2, 0.0))
    r = lax.broadcasted_iota(jnp.int32, (TILE_ROWS, TM), 0)
    perm = jnp.where(r == slot1.astype(jnp.int32), 1.0, jnp.where(r == slot2.astype(jnp.int32), 1.0, 0.0))
    buf = t % 2
    stage_ref[buf] = _dot(perm.astype(BF16), hf_ref[...]).astype(BF16)

    def store(tile, b, start):
        def copy(local, glob, p):
            return pltpu.make_async_copy(stage_ref.at[b, pl.ds(local, p)], xs_ref.at[pl.ds(glob, p)], sem.at[b])

        _segment_copies(tile, seg_ref, len_ref, base_ref, copy, start=start)

    store(t, buf, start=True)

    @pl.when(t > 0)
    def _():
        store(t - 1, 1 - buf, start=False)

    @pl.when(t == pl.num_programs(0) - 1)
    def _():
        store(t, buf, start=False)
        _zero_unused_rows(gap_ref, na_ref, xs_ref, zero_ref, zsem)


def _dispatch(seg, seg_len, base, gaps, n_active, hf, route, tri, n_tiles_max):
    m, d = hf.shape
    nt = m // TM
    grid_spec = pltpu.PrefetchScalarGridSpec(
        num_scalar_prefetch=5,
        grid=(nt,),
        in_specs=[pl.BlockSpec((TM, d), lambda t, *_: (t, 0)),
                  pl.BlockSpec((1, N_EXPERTS, TM), lambda t, *_: (t, 0, 0)),
                  pl.BlockSpec((TM, TM), lambda t, *_: (0, 0))],
        out_specs=[pl.BlockSpec(memory_space=pl.ANY),
                   pl.BlockSpec((1, N_EXPERTS, TM), lambda t, *_: (t, 0, 0))],
        scratch_shapes=[pltpu.VMEM((2, TILE_ROWS, d), BF16), pltpu.VMEM((TM, d), BF16),
                        pltpu.SemaphoreType.DMA((2,)), pltpu.SemaphoreType.DMA(())],
    )
    return pl.pallas_call(
        _dispatch_kernel,
        grid_spec=grid_spec,
        out_shape=[jax.ShapeDtypeStruct((n_tiles_max * TM, d), BF16),
                   jax.ShapeDtypeStruct((nt, N_EXPERTS, TM), F32)],
        compiler_params=_cparams(1),
        name="moe_dispatch",
    )(seg, seg_len, base, gaps, n_active, hf, route, tri)


def _experts_kernel(te_ref, na_ref, x_ref, w1_ref, w3_ref, w2_ref, o_ref, acc_ref, *, nf):
    del te_ref
    i = pl.program_id(0)
    f = pl.program_id(1)

    active = i < na_ref[0]
    last = nf - 1

    def partial_out():
        x = x_ref[...]
        act = jax.nn.silu(_dot(x, w1_ref[0])) * _dot(x, w3_ref[0])
        return _dot(act.astype(BF16), w2_ref[0])

    @pl.when(jnp.logical_and(active, f == 0))
    def _():
        acc_ref[...] = partial_out()

    if nf > 2:
        @pl.when(jnp.logical_and(active, jnp.logical_and(f > 0, f < last)))
        def _():
            acc_ref[...] += partial_out()

    @pl.when(jnp.logical_and(active, f == last))
    def _():
        o_ref[...] = (acc_ref[...] + partial_out()).astype(BF16)

    @pl.when(jnp.logical_and(i >= na_ref[0], f == last))
    def _():
        o_ref[...] = jnp.zeros_like(o_ref)


def _experts(tile_expert, n_active, x_sorted, w1, w3, w2, layer, tf):
    rows, d = x_sorted.shape
    d_ff = w1.shape[3]
    n_tiles, nf = rows // TM, d_ff // tf
    row_blk = lambda i, f, te, na: (jnp.minimum(i, na[0] - 1), 0)
    chunk = lambda i, f, na: jnp.where(i < na[0], f, nf - 1)
    grid_spec = pltpu.PrefetchScalarGridSpec(
        num_scalar_prefetch=2,
        grid=(n_tiles, nf),
        in_specs=[pl.BlockSpec((TM, d), row_blk),
                  pl.BlockSpec((None, 1, d, tf), lambda i, f, te, na: (layer, te[i], 0, chunk(i, f, na))),
                  pl.BlockSpec((None, 1, d, tf), lambda i, f, te, na: (layer, te[i], 0, chunk(i, f, na))),
                  pl.BlockSpec((None, 1, tf, d), lambda i, f, te, na: (layer, te[i], chunk(i, f, na), 0))],
        out_specs=pl.BlockSpec((TM, d), lambda i, f, te, na: (i, 0)),
        scratch_shapes=[pltpu.VMEM((TM, d), F32)],
    )
    assert nf >= 2
    return pl.pallas_call(
        functools.partial(_experts_kernel, nf=nf),
        grid_spec=grid_spec,
        out_shape=jax.ShapeDtypeStruct((rows, d), BF16),
        compiler_params=_cparams(2),
        name="moe_experts",
    )(tile_expert, n_active, x_sorted, w1, w3, w2)


def _combine_kernel(seg_ref, len_ref, base_ref, h_ref, slot_ref, gate_ref, y_ref, o_ref, ybuf_ref, sem):
    t = pl.program_id(0)
    buf = t % 2

    def fetch(tile, b, start):
        def copy(local, glob, p):
            return pltpu.make_async_copy(y_ref.at[pl.ds(glob, p)], ybuf_ref.at[b, pl.ds(local, p)], sem.at[b])

        if start:
            ybuf_ref[b, 2 * TM:, :] = jnp.zeros((TILE_ROWS - 2 * TM, ybuf_ref.shape[2]), BF16)
        _segment_copies(tile, seg_ref, len_ref, base_ref, copy, start=start)

    @pl.when(t == 0)
    def _():
        fetch(t, buf, start=True)

    @pl.when(t + 1 < pl.num_programs(0))
    def _():
        fetch(t + 1, 1 - buf, start=True)

    fetch(t, buf, start=False)

    y = ybuf_ref[buf]
    lane = lax.broadcasted_iota(jnp.int32, (TM, TILE_ROWS), 1)
    out = h_ref[...]
    for k in range(2):
        onehot = jnp.where(lane == slot_ref[:, k:k + 1], 1.0, 0.0).astype(BF16)
        out = out + gate_ref[:, k:k + 1] * _dot(onehot, y)
    o_ref[...] = out


def _combine(seg, seg_len, base, h2d, slots, gates, y_sorted):
    m, d = h2d.shape
    nt = m // TM
    grid_spec = pltpu.PrefetchScalarGridSpec(
        num_scalar_prefetch=3,
        grid=(nt,),
        in_specs=[pl.BlockSpec((TM, d), lambda t, *_: (t, 0)),
                  pl.BlockSpec((TM, 2), lambda t, *_: (t, 0)),
                  pl.BlockSpec((TM, 2), lambda t, *_: (t, 0)),
                  pl.BlockSpec(memory_space=pl.ANY)],
        out_specs=pl.BlockSpec((TM, d), lambda t, *_: (t, 0)),
        scratch_shapes=[pltpu.VMEM((2, TILE_ROWS, d), BF16), pltpu.SemaphoreType.DMA((2,))],
    )
    return pl.pallas_call(
        _combine_kernel,
        grid_spec=grid_spec,
        out_shape=jax.ShapeDtypeStruct((m, d), F32),
        compiler_params=_cparams(1),
        name="moe_combine",
    )(seg, seg_len, base, h2d, slots, gates, y_sorted)


def _moe_tables(counts, n_tiles_max):
    pad = (counts + SEG_ALIGN - 1) // SEG_ALIGN * SEG_ALIGN
    seg = jnp.cumsum(pad, axis=1) - pad
    used = jnp.sum(pad, axis=0)
    region = (used + TM - 1) // TM * TM
    region_start = jnp.cumsum(region) - region
    base = region_start[None, :] + jnp.cumsum(pad, axis=0) - pad
    gaps = jnp.concatenate([region_start + used, region - used])
    tiles_end = jnp.cumsum(region // TM)
    n_active = tiles_end[-1]
    tile = jnp.minimum(jnp.arange(n_tiles_max), n_active - 1)
    tile_expert = jnp.sum(tile[:, None] >= tiles_end[None, :], axis=1)
    flat = lambda a: a.reshape(-1).astype(jnp.int32)
    return flat(seg), flat(pad), flat(base), flat(gaps), flat(tile_expert), flat(n_active)


def _moe_layer(h, attn, w_o, g, w_router, w1, w3, w2, layer, tf):
    bsz, t, d = h.shape
    m = bsz * t
    nt = m // TM
    n_tiles_max = -(-(nt * (TILE_ROWS - SEG_ALIGN) + N_EXPERTS * (TM - 1)) // TM)
    attn2d = None if attn is None else attn.reshape(m, attn.shape[2])
    h2d, hf, route, counts = _router(h.reshape(m, d), attn2d, w_o, g, w_router.T)
    seg, seg_len, base, gaps, tile_expert, n_active = _moe_tables(counts[:, :, 0].astype(jnp.int32), n_tiles_max)
    tri = (jnp.arange(TM)[:, None] < jnp.arange(TM)[None, :]).astype(BF16)
    x_sorted, slots = _dispatch(seg, seg_len, base, gaps, n_active, hf, route, tri, n_tiles_max)
    y_sorted = _experts(tile_expert, n_active, x_sorted, w1, w3, w2, layer, tf)
    to_cols = lambda a: a.transpose(0, 2, 1).reshape(m, 2)
    out = _combine(seg, seg_len, base, h2d, to_cols(slots[:, :2, :]).astype(jnp.int32),
                   to_cols(route[:, 2:4, :]), y_sorted)
    return out.reshape(bsz, t, d)


def _kv_kernel(h_ref, g_ref, wd_ref, gl_ref, cs_ref, wk_ref, wv_ref, k_ref, v_ref):
    hs = _rms(h_ref[0], g_ref[...]).astype(BF16)
    ckv = _dot(hs, wd_ref[...])
    c_lat = _rms(ckv[:, :KV_LORA], gl_ref[...])
    pe = ckv[:, KV_LORA:KV_LORA + QK_ROPE]
    pe_rot = ckv[:, KV_LORA + QK_ROPE:]
    k_pe = pe * cs_ref[:, :QK_ROPE] + pe_rot * cs_ref[:, QK_ROPE:]
    lat = jnp.concatenate([c_lat, k_pe], axis=1).astype(BF16)
    k_ref[0] = _dot(lat, wk_ref[...]).astype(BF16)
    v_ref[0] = _dot(lat[:, :KV_LORA], wv_ref[...]).astype(BF16)


def _kv_proj(h, g_src, w_down_aug, g_latent, cs_k, w_k, w_v):
    bsz, t, d = h.shape
    nt = t // TT
    full = lambda shape: pl.BlockSpec(shape, lambda b, i: (0,) * len(shape))
    return pl.pallas_call(
        _kv_kernel,
        grid=(bsz, nt),
        in_specs=[pl.BlockSpec((1, TT, d), lambda b, i: (b, i, 0)), full((1, d)), full(w_down_aug.shape),
                  full((1, KV_LORA)), pl.BlockSpec((TT, 2 * QK_ROPE), lambda b, i: (i, 0)),
                  full(w_k.shape), full(w_v.shape)],
        out_specs=[pl.BlockSpec((1, TT, N_HEADS * HEAD_PAD), lambda b, i: (b, i, 0)),
                   pl.BlockSpec((1, TT, N_HEADS * V_HEAD), lambda b, i: (b, i, 0))],
        out_shape=[jax.ShapeDtypeStruct((bsz, t, N_HEADS * HEAD_PAD), BF16),
                   jax.ShapeDtypeStruct((bsz, t, N_HEADS * V_HEAD), BF16)],
        compiler_params=_cparams(2),
        name="kv_proj",
    )(h, g_src, w_down_aug, g_latent, cs_k, w_k, w_v)


def _q_kernel(h_ref, g_ref, wdq_ref, gq_ref, wuq_ref, ct_ref, st_ref, q_ref):
    hn = _rms(h_ref[0], g_ref[...]).astype(BF16)
    c_q = _rms(_dot(hn, wdq_ref[...]), gq_ref[...]).astype(BF16)
    q = _dot(c_q, wuq_ref[...])
    width = q.shape[1]
    ct = jnp.tile(ct_ref[...], (1, N_HEADS))
    st = jnp.tile(st_ref[...], (1, N_HEADS))
    q_ref[0] = (q * ct + pltpu.roll(q, width - QK_ROPE, axis=1) * st).astype(BF16)


def _q_proj(h, g, w_dq, g_q, w_uq_aug, ctab, stab):
    bsz, t, d = h.shape
    nt = t // TT
    full = lambda shape: pl.BlockSpec(shape, lambda b, i: (0,) * len(shape))
    tab = pl.BlockSpec((TT, HEAD_PAD), lambda b, i: (i, 0))
    return pl.pallas_call(
        _q_kernel,
        grid=(bsz, nt),
        in_specs=[pl.BlockSpec((1, TT, d), lambda b, i: (b, i, 0)), full((1, d)), full(w_dq.shape),
                  full((1, Q_LORA)), full(w_uq_aug.shape), tab, tab],
        out_specs=pl.BlockSpec((1, TT, N_HEADS * HEAD_PAD), lambda b, i: (b, i, 0)),
        out_shape=jax.ShapeDtypeStruct((bsz, t, N_HEADS * HEAD_PAD), BF16),
        compiler_params=_cparams(2),
        name="q_proj",
    )(h, g, w_dq, g_q, w_uq_aug, ctab, stab)


def _attn_kernel(q_ref, k_ref, v_ref, o_ref, s_ref, vt_ref):
    n_qblk = (q_ref.shape[1] - N_META) // Q_BLOCK
    heads = (0, 1)
    qk_sl = [slice(hh * HEAD_PAD, (hh + 1) * HEAD_PAD) for hh in heads]
    pair_v = 2 * V_HEAD

    def v_ext(vv):
        n = vv.shape[0]
        head0 = lax.broadcasted_iota(jnp.int32, (n, pair_v), 1) < V_HEAD
        sel0 = jnp.where(head0, 1.0, 0.0).astype(BF16)
        sel1 = jnp.where(head0, 0.0, 1.0).astype(BF16)
        r = lax.broadcasted_iota(jnp.int32, (2 * n, pair_v), 0)
        c = lax.broadcasted_iota(jnp.int32, (2 * n, pair_v), 1)
        ones = jnp.where(c == jnp.where(r < n, 0, 1), 1.0, 0.0).astype(BF16)
        return jnp.concatenate([jnp.concatenate([vv * sel0, vv * sel1], axis=0), ones], axis=1)

    def normalise(acc):
        head0 = lax.broadcasted_iota(jnp.int32, (acc.shape[0], pair_v), 1) < V_HEAD
        inv = jnp.where(head0, 1.0 / acc[:, pair_v:pair_v + 1], 1.0 / acc[:, pair_v + 1:pair_v + 2])
        return (acc[:, :pair_v] * inv).astype(BF16)

    k_meta = [k_ref[0, 0:N_META, qk_sl[hh]] for hh in heads]
    v_meta = v_ext(v_ref[0, 0:N_META, :])

    ps = []
    for hh in heads:
        s = _dot_nt(q_ref[0, 0:N_META, qk_sl[hh]], k_meta[hh])
        ps.append(jnp.exp2(s - jnp.max(s, axis=1, keepdims=True)))
    o_ref[0, 0:N_META, :] = normalise(_dot(jnp.concatenate(ps, axis=1).astype(BF16), v_meta))

    kc = lax.broadcasted_iota(jnp.int32, (Q_BLOCK, Q_BLOCK), 0) // CHUNK
    qc = lax.broadcasted_iota(jnp.int32, (Q_BLOCK, Q_BLOCK), 1) // CHUNK
    diag_mask = kc <= qc
    eye = jnp.where(lax.broadcasted_iota(jnp.int32, (pair_v, pair_v), 0)
                    == lax.broadcasted_iota(jnp.int32, (pair_v, pair_v), 1), 1.0, 0.0).astype(BF16)

    def vt_ext(vv):
        vt = _dot_nt(eye, vv).astype(BF16)
        ones = jnp.ones((SEG_ALIGN, vv.shape[0]), BF16)
        return [jnp.concatenate([vt[hh * V_HEAD:(hh + 1) * V_HEAD], ones], axis=0) for hh in heads]

    vt_meta = vt_ext(v_ref[0, 0:N_META, :])
    n_kblk = n_qblk
    for j in range(n_kblk):
        c0 = N_META + j * Q_BLOCK
        blk = vt_ext(v_ref[0, c0:c0 + Q_BLOCK, :])
        for hh in heads:
            vt_ref[j, hh] = blk[hh]

    def fold(s):
        return jnp.max(s.reshape(s.shape[0] // SUBLANES, SUBLANES, s.shape[1]), axis=0)

    for i in range(n_qblk):
        r0 = N_META + i * Q_BLOCK
        qs = [q_ref[0, r0:r0 + Q_BLOCK, qk_sl[hh]] for hh in heads]
        s_meta = [_dot_nt(k_meta[hh], qs[hh]) for hh in heads]
        mf = [None, None]
        for j in range(i + 1):
            c0 = N_META + j * Q_BLOCK
            for hh in heads:
                s = _dot_nt(k_ref[0, c0:c0 + Q_BLOCK, qk_sl[hh]], qs[hh])
                if j == i:
                    s = jnp.where(diag_mask, s, MASK_VALUE)
                s_ref[j, hh] = s
                mf[hh] = fold(s) if mf[hh] is None else jnp.maximum(mf[hh], fold(s))
        outs = []
        for hh in heads:
            m = jnp.maximum(jnp.max(mf[hh], axis=0, keepdims=True), jnp.max(s_meta[hh], axis=0, keepdims=True))
            acc = _dot(vt_meta[hh], jnp.exp2(s_meta[hh] - m).astype(BF16))
            for j in range(i + 1):
                acc = acc + _dot(vt_ref[j, hh], jnp.exp2(s_ref[j, hh] - m).astype(BF16))
            outs.append(acc[:V_HEAD] * (1.0 / acc[V_HEAD:V_HEAD + 1]))
        o_ref[0, r0:r0 + Q_BLOCK, :] = jnp.concatenate(outs, axis=0).T.astype(BF16)


def _attention(q, k, v):
    bsz, t, _ = q.shape
    n_blk = (t - N_META) // Q_BLOCK
    qk_spec = pl.BlockSpec((1, t, 2 * HEAD_PAD), lambda b, p: (b, 0, p))
    v_spec = pl.BlockSpec((1, t, 2 * V_HEAD), lambda b, p: (b, 0, p))
    return pl.pallas_call(
        _attn_kernel,
        grid=(bsz, N_HEADS // 2),
        in_specs=[qk_spec, qk_spec, v_spec],
        out_specs=v_spec,
        out_shape=jax.ShapeDtypeStruct((bsz, t, N_HEADS * V_HEAD), BF16),
        scratch_shapes=[pltpu.VMEM((n_blk, 2, Q_BLOCK, Q_BLOCK), F32),
                        pltpu.VMEM((n_blk, 2, V_HEAD + SEG_ALIGN, Q_BLOCK), BF16)],
        compiler_params=_cparams(2),
        name="attention",
    )(q, k, v)


def _final_kernel(h_ref, g_ref, o_ref):
    o_ref[0] = _rms(h_ref[0, N_META:, :], g_ref[...])


def _final_norm(h, g):
    bsz, t, d = h.shape
    return pl.pallas_call(
        _final_kernel,
        grid=(bsz,),
        in_specs=[pl.BlockSpec((1, t, d), lambda b: (b, 0, 0)), pl.BlockSpec((1, d), lambda b: (0, 0))],
        out_specs=pl.BlockSpec((1, t - N_META, d), lambda b: (b, 0, 0)),
        out_shape=jax.ShapeDtypeStruct((bsz, t - N_META, d), F32),
        compiler_params=_cparams(1),
        name="final_norm",
    )(h, g)


def _rot_cols(w):
    half = w.shape[1] // 2
    return jnp.concatenate([-w[:, half:], w[:, :half]], axis=1)


def _gate_pairs(w_a, w_i):
    def bd(w):
        z = jnp.zeros((RNN_BLOCK, RNN_BLOCK), w.dtype)
        return jnp.stack([jnp.block([[w[2 * p], z], [z, w[2 * p + 1]]]) for p in range(N_RNN_BLOCKS // 2)])
    return jnp.concatenate([bd(w_a), bd(w_i)], axis=2).astype(BF16)


def _q_up_aug(w_uq):
    w = w_uq.reshape(Q_LORA, N_HEADS, QK_NOPE + QK_ROPE)
    pe = w[:, :, QK_NOPE:]
    rot = jnp.concatenate([-pe[:, :, QK_ROPE // 2:], pe[:, :, :QK_ROPE // 2]], axis=2)
    return jnp.concatenate([w, rot], axis=2).reshape(Q_LORA, N_HEADS * HEAD_PAD).astype(BF16)


def _kv_up_aug(w_up):
    w = w_up.reshape(KV_LORA, N_HEADS, QK_NOPE + V_HEAD)
    pad = HEAD_PAD - QK_NOPE
    w_k_top = jnp.concatenate([w[:, :, :QK_NOPE], jnp.zeros((KV_LORA, N_HEADS, pad), w.dtype)], axis=2)
    eye = jnp.concatenate([jnp.zeros((QK_ROPE, QK_NOPE), w.dtype), jnp.eye(QK_ROPE, dtype=w.dtype),
                           jnp.zeros((QK_ROPE, pad - QK_ROPE), w.dtype)], axis=1)
    w_k_bot = jnp.broadcast_to(eye[:, None, :], (QK_ROPE, N_HEADS, HEAD_PAD))
    w_k = jnp.concatenate([w_k_top, w_k_bot], axis=0).reshape(KV_LORA + QK_ROPE, N_HEADS * HEAD_PAD)
    w_v = w[:, :, QK_NOPE:].reshape(KV_LORA, N_HEADS * V_HEAD)
    return w_k.astype(BF16), w_v.astype(BF16)


def _rope_tables(t):
    inv_freq = ROPE_THETA ** (-jnp.arange(0, QK_ROPE, 2, dtype=F32) / QK_ROPE)
    ang = jnp.arange(t, dtype=F32)[:, None] * inv_freq[None, :]
    cos = jnp.tile(jnp.cos(ang), (1, 2))
    sin = jnp.tile(jnp.sin(ang), (1, 2))
    cs_k = jnp.concatenate([cos, sin], axis=1)
    ones = jnp.ones((t, QK_NOPE), F32)
    zeros = jnp.zeros((t, HEAD_PAD - QK_NOPE - QK_ROPE), F32)
    ctab = SCORE_SCALE * jnp.concatenate([ones, cos, zeros], axis=1)
    stab = SCORE_SCALE * jnp.concatenate([0.0 * ones, sin, zeros], axis=1)
    return cs_k, ctab, stab


def kernel(x, meta_tokens, norm_mix, norm_ffn, norm_final, rnn_w_in, rnn_conv_w, rnn_conv_b, rnn_w_a, rnn_b_a, rnn_w_i, rnn_b_i, rnn_lambda, rnn_w_out, kv_norm_src, kv_w_down, kv_latent_norm, kv_w_up, q_w_down, q_latent_norm, q_w_up, attn_w_out, ffn_w1, ffn_w3, ffn_w2, moe_router, moe_w1, moe_w3, moe_w2):
    bsz = x.shape[0]
    meta = jnp.broadcast_to(meta_tokens.astype(x.dtype)[None], (bsz, N_META, x.shape[-1]))
    h = jnp.concatenate([meta, x], axis=1)
    t = h.shape[1]
    assert t % TT == 0 and (t - N_META) % Q_BLOCK == 0
    row = lambda v: v.reshape(1, -1)
    cs_k, ctab, stab = _rope_tables(t)

    ffn_w = [w.astype(BF16) for w in (ffn_w1, ffn_w3, ffn_w2)]
    moe_w = [w.astype(BF16) for w in (moe_w1, moe_w3, moe_w2)]
    k = v = attn = w_o = None
    for layer in range(DEPTH):
        if layer < N_A_LAYERS:
            a = layer
            h = _rnn_layer(h, row(norm_mix[layer]), rnn_w_in[a].astype(BF16), rnn_conv_w[a], row(rnn_conv_b[a]),
                           _gate_pairs(rnn_w_a[a], rnn_w_i[a]), row(rnn_b_a[a]), row(rnn_b_i[a]),
                           row(rnn_lambda[a]), rnn_w_out[a].astype(BF16))
        else:
            b = layer - N_A_LAYERS
            if b == 0:
                w_down_aug = jnp.concatenate([kv_w_down, _rot_cols(kv_w_down[:, KV_LORA:])], axis=1).astype(BF16)
                w_k, w_v = _kv_up_aug(kv_w_up)
                k, v = _kv_proj(h, row(kv_norm_src), w_down_aug, row(kv_latent_norm), cs_k, w_k, w_v)
            q = _q_proj(h, row(norm_mix[layer]), q_w_down[b].astype(BF16), row(q_latent_norm[b]),
                        _q_up_aug(q_w_up[b]), ctab, stab)
            attn, w_o = _attention(q, k, v), attn_w_out[b].astype(BF16)
        if layer % 2 == 0:
            h = _ffn_layer(h, attn, w_o, row(norm_ffn[layer]), *ffn_w, e=layer // 2)
        else:
            h = _moe_layer(h, attn, w_o, row(norm_ffn[layer]), moe_router[layer // 2], *moe_w,
                           layer=layer // 2, tf=1792)
        attn = None
    return _final_norm(h, row(norm_final))
```

```python
import functools
import math

import jax
import jax.numpy as jnp
from jax import lax
from jax.experimental import pallas as pl
from jax.experimental.pallas import tpu as pltpu

D_MODEL = 1024
N_META = 16
CHUNK = 64
NORM_EPS = 1e-6
DEPTH = 4
N_A_LAYERS = DEPTH // 2

D_RNN = D_MODEL
N_RNN_BLOCKS = 8
RNN_BLOCK = D_RNN // N_RNN_BLOCKS
CONV_WIDTH = 4
LRU_C = 8.0

N_HEADS = 16
QK_NOPE = 64
QK_ROPE = 32
V_HEAD = 64
Q_LORA = 384
KV_LORA = 256
ROPE_THETA = 10000.0
ATTN_SCALE = 1.0 / math.sqrt(QK_NOPE + QK_ROPE)
MASK_VALUE = -1e30
SCORE_SCALE = ATTN_SCALE * math.log2(math.e)

N_EXPERTS = 8

LANES = 128
SUBLANES = 8
VMEM_LIMIT = 56 * 1024 * 1024

HEAD_PAD = 128
Q_BLOCK = 512
TT = 688
FF_CHUNK = 1024
SCAN_GROUPS = 2
TM = 512
SEG_ALIGN = 16
SEG_SIZES = (512, 256, 128, 64, 32, 16)
TILE_ROWS = 2 * TM + N_EXPERTS * SEG_ALIGN
F32 = jnp.float32
BF16 = jnp.bfloat16


def _cparams(n_axes):
    return pltpu.CompilerParams(dimension_semantics=("arbitrary",) * n_axes,
                                vmem_limit_bytes=VMEM_LIMIT)


def _rms(x, g):
    return x * lax.rsqrt(jnp.mean(x * x, axis=-1, keepdims=True) + NORM_EPS) * g


def _sigmoid(x):
    return 0.5 * jnp.tanh(0.5 * x) + 0.5


def _dot(a, b):
    return jnp.dot(a, b, preferred_element_type=F32)


def _dot_nt(a, b):
    return lax.dot_general(a, b, (((1,), (1,)), ((), ())), preferred_element_type=F32)


def _rnn_kernel(*refs, prepend_meta):
    if prepend_meta:
        (h_ref, meta_ref, g_ref, win_ref, cw_ref, cb_ref, wg_ref, ba_ref, bi_ref, lam_ref, wout_ref,
         o_ref, xpad_ref, yg_ref, a_ref, u_ref, carry_ref, head_ref) = refs
    else:
        (h_ref, g_ref, win_ref, cw_ref, cb_ref, wg_ref, ba_ref, bi_ref, lam_ref, wout_ref,
         o_ref, xpad_ref, yg_ref, a_ref, u_ref, carry_ref) = refs
    t = pl.program_id(1)
    tt = h_ref.shape[1]

    @pl.when(t == 0)
    def _():
        xpad_ref[0:SUBLANES, :] = jnp.zeros((SUBLANES, D_RNN), F32)
        carry_ref[...] = jnp.zeros_like(carry_ref)
        if prepend_meta:
            head_ref[...] = meta_ref[...]

    if prepend_meta:
        x = jnp.concatenate([head_ref[...], h_ref[0, :tt - N_META, :]], axis=0)

        @pl.when(t < pl.num_programs(1) - 1)
        def _():
            head_ref[...] = h_ref[0, tt - N_META:, :]
    else:
        x = h_ref[0]
    hn = _rms(x, g_ref[...]).astype(BF16)
    yg_ref[...] = jax.nn.gelu(_dot(hn, win_ref[:, :D_RNN]), approximate=True)
    xpad_ref[SUBLANES:, :] = _dot(hn, win_ref[:, D_RNN:])

    xc = cb_ref[...] + cw_ref[CONV_WIDTH - 1:CONV_WIDTH, :] * xpad_ref[SUBLANES:, :]
    for j in range(CONV_WIDTH - 1):
        off = SUBLANES - (CONV_WIDTH - 1) + j
        xc = xc + cw_ref[j:j + 1, :] * xpad_ref[off:off + tt, :]
    xpad_ref[0:SUBLANES, :] = xpad_ref[tt:tt + SUBLANES, :]

    xcb = xc.astype(BF16)
    log_coef = -LRU_C * jax.nn.softplus(-lam_ref[...])
    pair = 2 * RNN_BLOCK
    for p in range(N_RNN_BLOCKS // 2):
        sl = slice(p * pair, (p + 1) * pair)
        gates = _dot(xcb[:, sl], wg_ref[p])
        gate_r = _sigmoid(gates[:, :pair] + ba_ref[:, sl])
        gate_i = _sigmoid(gates[:, pair:] + bi_ref[:, sl])
        log_a = log_coef[:, sl] * gate_r
        a_ref[:, sl] = jnp.exp(log_a)
        th = jnp.tanh(log_a)
        u_ref[:, sl] = jnp.sqrt(-2.0 * th / (1.0 - th)) * (gate_i * xc[:, sl])

    row = lax.broadcasted_iota(jnp.int32, (SUBLANES, D_RNN), 0)

    def groups(i, carry):
        local = []
        for g in range(SCAN_GROUPS):
            r0 = pl.multiple_of((i * SCAN_GROUPS + g) * SUBLANES, SUBLANES)
            a = a_ref[pl.ds(r0, SUBLANES), :]
            u = u_ref[pl.ds(r0, SUBLANES), :]
            for s in (1, 2, 4):
                keep = row >= s
                u = jnp.where(keep, a * pltpu.roll(u, s, axis=0) + u, u)
                a = jnp.where(keep, a * pltpu.roll(a, s, axis=0), a)
            local.append((r0, a, u))
        for r0, a, u in local:
            hs = a * carry + u
            u_ref[pl.ds(r0, SUBLANES), :] = hs
            carry = hs[SUBLANES - 1:SUBLANES, :]
        return carry

    carry_ref[...] = lax.fori_loop(0, tt // (SCAN_GROUPS * SUBLANES), groups, carry_ref[...])

    y = (u_ref[...] * yg_ref[...]).astype(BF16)
    o_ref[0] = x + _dot(y, wout_ref[...])


def _rnn_layer(h, meta, g, w_in, conv_w, conv_b, w_gate, b_a, b_i, lam, w_out):
    bsz, t, d = h.shape
    if meta is not None:
        t += N_META
    nt = t // TT
    full = lambda shape: pl.BlockSpec(shape, lambda b, i: (0,) * len(shape))
    tile = pl.BlockSpec((1, TT, d), lambda b, i: (b, i, 0))
    meta_args, meta_specs, meta_scratch = (), [], []
    if meta is not None:
        meta_args, meta_specs = (meta,), [full((N_META, d))]
        meta_scratch = [pltpu.VMEM((N_META, d), F32)]
    return pl.pallas_call(
        functools.partial(_rnn_kernel, prepend_meta=meta is not None),
        grid=(bsz, nt),
        in_specs=[tile] + meta_specs +
                 [full((1, d)), full((d, 2 * D_RNN)), full((CONV_WIDTH, D_RNN)), full((1, D_RNN)),
                  full(w_gate.shape), full((1, D_RNN)), full((1, D_RNN)), full((1, D_RNN)),
                  full((D_RNN, d))],
        out_specs=tile,
        out_shape=jax.ShapeDtypeStruct((bsz, t, d), F32),
        scratch_shapes=[pltpu.VMEM((TT + SUBLANES, D_RNN), F32), pltpu.VMEM((TT, D_RNN), F32),
                        pltpu.VMEM((TT, D_RNN), F32), pltpu.VMEM((TT, D_RNN), F32),
                        pltpu.VMEM((1, D_RNN), F32)] + meta_scratch,
        compiler_params=_cparams(2),
        name="rnn_mixer",
    )(h, *meta_args, g, w_in, conv_w, conv_b, w_gate, b_a, b_i, lam, w_out)


def _ffn_kernel(*refs, with_attn):
    if with_attn:
        h_ref, a_ref, wo_ref, g_ref, w1_ref, w3_ref, w2_ref, o_ref = refs
    else:
        h_ref, g_ref, w1_ref, w3_ref, w2_ref, o_ref = refs
    out = h_ref[0]
    if with_attn:
        out = out + _dot(a_ref[0], wo_ref[...])
    hf = _rms(out, g_ref[...]).astype(BF16)
    d_ff = w1_ref.shape[1]
    for c0 in range(0, d_ff, FF_CHUNK):
        c1 = min(c0 + FF_CHUNK, d_ff)
        act = jax.nn.silu(_dot(hf, w1_ref[:, c0:c1])) * _dot(hf, w3_ref[:, c0:c1])
        out = out + _dot(act.astype(BF16), w2_ref[c0:c1, :])
    o_ref[0] = out


def _ffn_layer(h, attn, w_o, g, w1, w3, w2, e):
    bsz, t, d = h.shape
    d_ff = w1.shape[2]
    nt = t // TT
    tile = pl.BlockSpec((1, TT, d), lambda b, i: (b, i, 0))
    resident = lambda shape, idx: pl.BlockSpec(shape, lambda b, i: idx, pipeline_mode=pl.Buffered(1))
    attn_args, attn_specs = (), []
    if attn is not None:
        attn_args = (attn, w_o)
        attn_specs = [pl.BlockSpec((1, TT, attn.shape[2]), lambda b, i: (b, i, 0)),
                      resident(w_o.shape, (0, 0))]
    return pl.pallas_call(
        functools.partial(_ffn_kernel, with_attn=attn is not None),
        grid=(bsz, nt),
        in_specs=[tile] + attn_specs +
                 [pl.BlockSpec((1, d), lambda b, i: (0, 0)),
                  resident((None, d, d_ff), (e, 0, 0)),
                  resident((None, d, d_ff), (e, 0, 0)),
                  resident((None, d_ff, d), (e, 0, 0))],
        out_specs=tile,
        out_shape=jax.ShapeDtypeStruct(h.shape, F32),
        compiler_params=_cparams(2),
        name="ffn_dense",
    )(h, *attn_args, g, w1, w3, w2)


def _router_kernel(*refs, with_attn):
    if with_attn:
        h_ref, a_ref, wo_ref, g_ref, wrt_ref, hmid_ref, hf_ref, route_ref, cnt_ref = refs
        hmid = h_ref[...] + _dot(a_ref[...], wo_ref[...])
        hmid_ref[...] = hmid
    else:
        h_ref, g_ref, wrt_ref, hf_ref, route_ref, cnt_ref = refs
        hmid = h_ref[...]
    hf = _rms(hmid, g_ref[...])
    hi = hf.astype(BF16)
    hf_ref[...] = hi
    lo = (hf - hi.astype(F32)).astype(BF16)
    w = wrt_ref[...]
    w_hi = w.astype(BF16)
    w_split = jnp.concatenate([w_hi, (w - w_hi.astype(F32)).astype(BF16)], axis=0)
    part_hi = _dot_nt(w_split, hi)
    logits = part_hi[:N_EXPERTS] + part_hi[N_EXPERTS:] + _dot_nt(w_split, lo)[:N_EXPERTS]
    eidx = lax.broadcasted_iota(jnp.int32, logits.shape, 0)
    m1 = jnp.max(logits, axis=0, keepdims=True)
    i1 = jnp.min(jnp.where(logits == m1, eidx, N_EXPERTS), axis=0, keepdims=True)
    rest = jnp.where(eidx == i1, -jnp.inf, logits)
    m2 = jnp.max(rest, axis=0, keepdims=True)
    i2 = jnp.min(jnp.where(rest == m2, eidx, N_EXPERTS), axis=0, keepdims=True)
    ex = jnp.exp(m2 - m1)
    inv = 1.0 / (1.0 + ex)
    sel = jnp.where(eidx == i1, 1.0, 0.0) + jnp.where(eidx == i2, 1.0, 0.0)
    cnt_ref[0] = jnp.broadcast_to(jnp.sum(sel, axis=1, keepdims=True), cnt_ref.shape[1:])
    route_ref[0] = jnp.where(eidx == 0, i1.astype(F32), jnp.where(eidx == 1, i2.astype(F32),
                             jnp.where(eidx == 2, inv, jnp.where(eidx == 3, ex * inv, 0.0))))


def _router(h2d, attn2d, w_o, g, w_router_t):
    m, d = h2d.shape
    nt = m // TM
    rows = pl.BlockSpec((TM, d), lambda t: (t, 0))
    attn_args, attn_specs, hmid_specs, hmid_shapes = (), [], [], []
    if attn2d is not None:
        attn_args = (attn2d, w_o)
        attn_specs = [pl.BlockSpec((TM, attn2d.shape[1]), lambda t: (t, 0)), pl.BlockSpec(w_o.shape, lambda t: (0, 0))]
        hmid_specs, hmid_shapes = [rows], [jax.ShapeDtypeStruct((m, d), F32)]
    outs = pl.pallas_call(
        functools.partial(_router_kernel, with_attn=attn2d is not None),
        grid=(nt,),
        in_specs=[rows] + attn_specs + [pl.BlockSpec((1, d), lambda t: (0, 0)),
                                        pl.BlockSpec((N_EXPERTS, d), lambda t: (0, 0))],
        out_specs=hmid_specs + [rows,
                                pl.BlockSpec((1, N_EXPERTS, TM), lambda t: (t, 0, 0)),
                                pl.BlockSpec((1, N_EXPERTS, LANES), lambda t: (t, 0, 0))],
        out_shape=hmid_shapes + [jax.ShapeDtypeStruct((m, d), BF16),
                                 jax.ShapeDtypeStruct((nt, N_EXPERTS, TM), F32),
                                 jax.ShapeDtypeStruct((nt, N_EXPERTS, LANES), F32)],
        compiler_params=_cparams(1),
        name="moe_router",
    )(h2d, *attn_args, g, w_router_t)
    return tuple(outs) if attn2d is not None else (h2d, *outs)


def _segment_copies(t, seg_ref, len_ref, base_ref, make_copy, start):
    for e in range(N_EXPERTS):
        n = len_ref[t * N_EXPERTS + e]
        local0 = seg_ref[t * N_EXPERTS + e]
        global0 = base_ref[t * N_EXPERTS + e]
        off = 0
        for p in SEG_SIZES:
            take = (n & p) != 0

            @pl.when(take)
            def _(off=off, p=p):
                cp = make_copy(pl.multiple_of(local0 + off, SEG_ALIGN), pl.multiple_of(global0 + off, SEG_ALIGN), p)
                if start:
                    cp.start()
                else:
                    cp.wait()

            off = off + jnp.where(take, p, 0)


def _zero_unused_rows(gap_ref, na_ref, xs_ref, zero_ref, zsem):
    zero_ref[...] = jnp.zeros_like(zero_ref)
    n_tiles = xs_ref.shape[0] // TM

    def gap_copies(start):
        for e in range(N_EXPERTS):
            n = gap_ref[N_EXPERTS + e]
            off = 0
            for p in SEG_SIZES[1:]:
                take = (n & p) != 0

                @pl.when(take)
                def _(off=off, p=p, e=e):
                    dst = pl.multiple_of(gap_ref[e] + off, SEG_ALIGN)
                    cp = pltpu.make_async_copy(zero_ref.at[pl.ds(0, p)], xs_ref.at[pl.ds(dst, p)], zsem)
                    cp.start() if start else cp.wait()

                off = off + jnp.where(take, p, 0)

    def tile_copy(i, carry, start):
        dst = pl.multiple_of(i * TM, TM)
        cp = pltpu.make_async_copy(zero_ref, xs_ref.at[pl.ds(dst, TM)], zsem)
        cp.start() if start else cp.wait()
        return carry

    gap_copies(True)
    lax.fori_loop(na_ref[0], n_tiles, functools.partial(tile_copy, start=True), 0)
    gap_copies(False)
    lax.fori_loop(na_ref[0], n_tiles, functools.partial(tile_copy, start=False), 0)


def _dispatch_kernel(seg_ref, len_ref, base_ref, gap_ref, na_ref, hf_ref, route_ref, tri_ref, xs_ref, slot_ref,
                     stage_ref, zero_ref, sem, zsem):
    t = pl.program_id(0)
    route = route_ref[0]
    eidx = lax.broadcasted_iota(jnp.int32, route.shape, 0)
    sel1 = eidx == route[0:1, :].astype(jnp.int32)
    sel2 = eidx == route[1:2, :].astype(jnp.int32)
    sel = jnp.where(sel1, 1.0, 0.0) + jnp.where(sel2, 1.0, 0.0)
    sel16 = jnp.concatenate([sel, jnp.zeros_like(sel)], axis=0).astype(BF16)
    rank = _dot(sel16, tri_ref[...])[:N_EXPERTS]
    seg = jnp.zeros(route.shape, jnp.int32)
    for e in range(N_EXPERTS):
        seg = jnp.where(eidx == e, seg_ref[t * N_EXPERTS + e], seg)
    slot = seg.astype(F32) + rank
    slot1 = jnp.sum(jnp.where(sel1, slot, 0.0), axis=0, keepdims=True)
    slot2 = jnp.sum(jnp.where(sel2, slot, 0.0), axis=0, keepdims=True)
    slot_ref[0] = jnp.where(eidx == 0, slot1, jnp.where(eidx == 1, slot2, 0.0))
    r = lax.broadcasted_iota(jnp.int32, (TILE_ROWS, TM), 0)
    perm = jnp.where(r == slot1.astype(jnp.int32), 1.0, jnp.where(r == slot2.astype(jnp.int32), 1.0, 0.0))
    buf = t % 2
    stage_ref[buf] = _dot(perm.astype(BF16), hf_ref[...]).astype(BF16)

    def store(tile, b, start):
        def copy(local, glob, p):
            return pltpu.make_async_copy(stage_ref.at[b, pl.ds(local, p)], xs_ref.at[pl.ds(glob, p)], sem.at[b])

        _segment_copies(tile, seg_ref, len_ref, base_ref, copy, start=start)

    store(t, buf, start=True)

    @pl.when(t > 0)
    def _():
        store(t - 1, 1 - buf, start=False)

    @pl.when(t == pl.num_programs(0) - 1)
    def _():
        store(t, buf, start=False)
        _zero_unused_rows(gap_ref, na_ref, xs_ref, zero_ref, zsem)


def _dispatch(seg, seg_len, base, gaps, n_active, hf, route, tri, n_tiles_max):
    m, d = hf.shape
    nt = m // TM
    grid_spec = pltpu.PrefetchScalarGridSpec(
        num_scalar_prefetch=5,
        grid=(nt,),
        in_specs=[pl.BlockSpec((TM, d), lambda t, *_: (t, 0)),
                  pl.BlockSpec((1, N_EXPERTS, TM), lambda t, *_: (t, 0, 0)),
                  pl.BlockSpec((TM, TM), lambda t, *_: (0, 0))],
        out_specs=[pl.BlockSpec(memory_space=pl.ANY),
                   pl.BlockSpec((1, N_EXPERTS, TM), lambda t, *_: (t, 0, 0))],
        scratch_shapes=[pltpu.VMEM((2, TILE_ROWS, d), BF16), pltpu.VMEM((TM, d), BF16),
                        pltpu.SemaphoreType.DMA((2,)), pltpu.SemaphoreType.DMA(())],
    )
    return pl.pallas_call(
        _dispatch_kernel,
        grid_spec=grid_spec,
        out_shape=[jax.ShapeDtypeStruct((n_tiles_max * TM, d), BF16),
                   jax.ShapeDtypeStruct((nt, N_EXPERTS, TM), F32)],
        compiler_params=_cparams(1),
        name="moe_dispatch",
    )(seg, seg_len, base, gaps, n_active, hf, route, tri)


def _experts_kernel(te_ref, na_ref, x_ref, w1_ref, w3_ref, w2_ref, o_ref, acc_ref, *, nf):
    del te_ref
    i = pl.program_id(0)
    f = pl.program_id(1)

    active = i < na_ref[0]
    last = nf - 1

    def partial_out():
        x = x_ref[...]
        act = jax.nn.silu(_dot(x, w1_ref[0])) * _dot(x, w3_ref[0])
        return _dot(act.astype(BF16), w2_ref[0])

    @pl.when(jnp.logical_and(active, f == 0))
    def _():
        acc_ref[...] = partial_out()

    if nf > 2:
        @pl.when(jnp.logical_and(active, jnp.logical_and(f > 0, f < last)))
        def _():
            acc_ref[...] += partial_out()

    @pl.when(jnp.logical_and(active, f == last))
    def _():
        o_ref[...] = (acc_ref[...] + partial_out()).astype(BF16)

    @pl.when(jnp.logical_and(i >= na_ref[0], f == last))
    def _():
        o_ref[...] = jnp.zeros_like(o_ref)


def _experts(tile_expert, n_active, x_sorted, w1, w3, w2, layer, tf):
    rows, d = x_sorted.shape
    d_ff = w1.shape[3]
    n_tiles, nf = rows // TM, d_ff // tf
    row_blk = lambda i, f, te, na: (jnp.minimum(i, na[0] - 1), 0)
    chunk = lambda i, f, na: jnp.where(i < na[0], f, nf - 1)
    grid_spec = pltpu.PrefetchScalarGridSpec(
        num_scalar_prefetch=2,
        grid=(n_tiles, nf),
        in_specs=[pl.BlockSpec((TM, d), row_blk),
                  pl.BlockSpec((None, 1, d, tf), lambda i, f, te, na: (layer, te[i], 0, chunk(i, f, na))),
                  pl.BlockSpec((None, 1, d, tf), lambda i, f, te, na: (layer, te[i], 0, chunk(i, f, na))),
                  pl.BlockSpec((None, 1, tf, d), lambda i, f, te, na: (layer, te[i], chunk(i, f, na), 0))],
        out_specs=pl.BlockSpec((TM, d), lambda i, f, te, na: (i, 0)),
        scratch_shapes=[pltpu.VMEM((TM, d), F32)],
    )
    assert nf >= 2
    return pl.pallas_call(
        functools.partial(_experts_kernel, nf=nf),
        grid_spec=grid_spec,
        out_shape=jax.ShapeDtypeStruct((rows, d), BF16),
        compiler_params=_cparams(2),
        name="moe_experts",
    )(tile_expert, n_active, x_sorted, w1, w3, w2)


def _combine_kernel(seg_ref, len_ref, base_ref, h_ref, slot_ref, gate_ref, y_ref, o_ref, ybuf_ref, sem):
    t = pl.program_id(0)
    buf = t % 2

    def fetch(tile, b, start):
        def copy(local, glob, p):
            return pltpu.make_async_copy(y_ref.at[pl.ds(glob, p)], ybuf_ref.at[b, pl.ds(local, p)], sem.at[b])

        if start:
            ybuf_ref[b, 2 * TM:, :] = jnp.zeros((TILE_ROWS - 2 * TM, ybuf_ref.shape[2]), BF16)
        _segment_copies(tile, seg_ref, len_ref, base_ref, copy, start=start)

    @pl.when(t == 0)
    def _():
        fetch(t, buf, start=True)

    @pl.when(t + 1 < pl.num_programs(0))
    def _():
        fetch(t + 1, 1 - buf, start=True)

    fetch(t, buf, start=False)

    y = ybuf_ref[buf]
    lane = lax.broadcasted_iota(jnp.int32, (TM, TILE_ROWS), 1)
    out = h_ref[...]
    for k in range(2):
        onehot = jnp.where(lane == slot_ref[:, k:k + 1], 1.0, 0.0).astype(BF16)
        out = out + gate_ref[:, k:k + 1] * _dot(onehot, y)
    o_ref[...] = out


def _combine(seg, seg_len, base, h2d, slots, gates, y_sorted):
    m, d = h2d.shape
    nt = m // TM
    grid_spec = pltpu.PrefetchScalarGridSpec(
        num_scalar_prefetch=3,
        grid=(nt,),
        in_specs=[pl.BlockSpec((TM, d), lambda t, *_: (t, 0)),
                  pl.BlockSpec((TM, 2), lambda t, *_: (t, 0)),
                  pl.BlockSpec((TM, 2), lambda t, *_: (t, 0)),
                  pl.BlockSpec(memory_space=pl.ANY)],
        out_specs=pl.BlockSpec((TM, d), lambda t, *_: (t, 0)),
        scratch_shapes=[pltpu.VMEM((2, TILE_ROWS, d), BF16), pltpu.SemaphoreType.DMA((2,))],
    )
    return pl.pallas_call(
        _combine_kernel,
        grid_spec=grid_spec,
        out_shape=jax.ShapeDtypeStruct((m, d), F32),
        compiler_params=_cparams(1),
        name="moe_combine",
    )(seg, seg_len, base, h2d, slots, gates, y_sorted)


def _moe_tables(counts, n_tiles_max):
    pad = (counts + SEG_ALIGN - 1) // SEG_ALIGN * SEG_ALIGN
    seg = jnp.cumsum(pad, axis=1) - pad
    used = jnp.sum(pad, axis=0)
    region = (used + TM - 1) // TM * TM
    region_start = jnp.cumsum(region) - region
    base = region_start[None, :] + jnp.cumsum(pad, axis=0) - pad
    gaps = jnp.concatenate([region_start + used, region - used])
    tiles_end = jnp.cumsum(region // TM)
    n_active = tiles_end[-1]
    tile = jnp.minimum(jnp.arange(n_tiles_max), n_active - 1)
    tile_expert = jnp.sum(tile[:, None] >= tiles_end[None, :], axis=1)
    flat = lambda a: a.reshape(-1).astype(jnp.int32)
    return flat(seg), flat(pad), flat(base), flat(gaps), flat(tile_expert), flat(n_active)


def _moe_layer(h, attn, w_o, g, w_router, w1, w3, w2, layer, tf):
    bsz, t, d = h.shape
    m = bsz * t
    nt = m // TM
    n_tiles_max = -(-(nt * (TILE_ROWS - SEG_ALIGN) + N_EXPERTS * (TM - 1)) // TM)
    attn2d = None if attn is None else attn.reshape(m, attn.shape[2])
    h2d, hf, route, counts = _router(h.reshape(m, d), attn2d, w_o, g, w_router.T)
    seg, seg_len, base, gaps, tile_expert, n_active = _moe_tables(counts[:, :, 0].astype(jnp.int32), n_tiles_max)
    tri = (jnp.arange(TM)[:, None] < jnp.arange(TM)[None, :]).astype(BF16)
    x_sorted, slots = _dispatch(seg, seg_len, base, gaps, n_active, hf, route, tri, n_tiles_max)
    y_sorted = _experts(tile_expert, n_active, x_sorted, w1, w3, w2, layer, tf)
    to_cols = lambda a: a.transpose(0, 2, 1).reshape(m, 2)
    out = _combine(seg, seg_len, base, h2d, to_cols(slots[:, :2, :]).astype(jnp.int32),
                   to_cols(route[:, 2:4, :]), y_sorted)
    return out.reshape(bsz, t, d)


def _kv_kernel(h_ref, g_ref, wd_ref, gl_ref, cs_ref, wk_ref, wv_ref, k_ref, v_ref):
    hs = _rms(h_ref[0], g_ref[...]).astype(BF16)
    ckv = _dot(hs, wd_ref[...])
    c_lat = _rms(ckv[:, :KV_LORA], gl_ref[...])
    pe = ckv[:, KV_LORA:KV_LORA + QK_ROPE]
    pe_rot = ckv[:, KV_LORA + QK_ROPE:]
    k_pe = pe * cs_ref[:, :QK_ROPE] + pe_rot * cs_ref[:, QK_ROPE:]
    lat = jnp.concatenate([c_lat, k_pe], axis=1).astype(BF16)
    k_ref[0] = _dot(lat, wk_ref[...]).astype(BF16)
    v_ref[0] = _dot(lat[:, :KV_LORA], wv_ref[...]).astype(BF16)


def _kv_proj(h, g_src, w_down_aug, g_latent, cs_k, w_k, w_v):
    bsz, t, d = h.shape
    nt = t // TT
    full = lambda shape: pl.BlockSpec(shape, lambda b, i: (0,) * len(shape))
    return pl.pallas_call(
        _kv_kernel,
        grid=(bsz, nt),
        in_specs=[pl.BlockSpec((1, TT, d), lambda b, i: (b, i, 0)), full((1, d)), full(w_down_aug.shape),
                  full((1, KV_LORA)), pl.BlockSpec((TT, 2 * QK_ROPE), lambda b, i: (i, 0)),
                  full(w_k.shape), full(w_v.shape)],
        out_specs=[pl.BlockSpec((1, TT, N_HEADS * HEAD_PAD), lambda b, i: (b, i, 0)),
                   pl.BlockSpec((1, TT, N_HEADS * V_HEAD), lambda b, i: (b, i, 0))],
        out_shape=[jax.ShapeDtypeStruct((bsz, t, N_HEADS * HEAD_PAD), BF16),
                   jax.ShapeDtypeStruct((bsz, t, N_HEADS * V_HEAD), BF16)],
        compiler_params=_cparams(2),
        name="kv_proj",
    )(h, g_src, w_down_aug, g_latent, cs_k, w_k, w_v)


def _q_kernel(h_ref, g_ref, wdq_ref, gq_ref, wuq_ref, ct_ref, st_ref, q_ref):
    hn = _rms(h_ref[0], g_ref[...]).astype(BF16)
    c_q = _rms(_dot(hn, wdq_ref[...]), gq_ref[...]).astype(BF16)
    q = _dot(c_q, wuq_ref[...])
    width = q.shape[1]
    ct = jnp.tile(ct_ref[...], (1, N_HEADS))
    st = jnp.tile(st_ref[...], (1, N_HEADS))
    q_ref[0] = (q * ct + pltpu.roll(q, width - QK_ROPE, axis=1) * st).astype(BF16)


def _q_proj(h, g, w_dq, g_q, w_uq_aug, ctab, stab):
    bsz, t, d = h.shape
    nt = t // TT
    full = lambda shape: pl.BlockSpec(shape, lambda b, i: (0,) * len(shape))
    tab = pl.BlockSpec((TT, HEAD_PAD), lambda b, i: (i, 0))
    return pl.pallas_call(
        _q_kernel,
        grid=(bsz, nt),
        in_specs=[pl.BlockSpec((1, TT, d), lambda b, i: (b, i, 0)), full((1, d)), full(w_dq.shape),
                  full((1, Q_LORA)), full(w_uq_aug.shape), tab, tab],
        out_specs=pl.BlockSpec((1, TT, N_HEADS * HEAD_PAD), lambda b, i: (b, i, 0)),
        out_shape=jax.ShapeDtypeStruct((bsz, t, N_HEADS * HEAD_PAD), BF16),
        compiler_params=_cparams(2),
        name="q_proj",
    )(h, g, w_dq, g_q, w_uq_aug, ctab, stab)


def _attn_kernel(q_ref, k_ref, v_ref, o_ref, s_ref, vt_ref):
    n_qblk = (q_ref.shape[1] - N_META) // Q_BLOCK
    heads = (0, 1)
    qk_sl = [slice(hh * HEAD_PAD, (hh + 1) * HEAD_PAD) for hh in heads]
    pair_v = 2 * V_HEAD

    def v_ext(vv):
        n = vv.shape[0]
        head0 = lax.broadcasted_iota(jnp.int32, (n, pair_v), 1) < V_HEAD
        sel0 = jnp.where(head0, 1.0, 0.0).astype(BF16)
        sel1 = jnp.where(head0, 0.0, 1.0).astype(BF16)
        r = lax.broadcasted_iota(jnp.int32, (2 * n, pair_v), 0)
        c = lax.broadcasted_iota(jnp.int32, (2 * n, pair_v), 1)
        ones = jnp.where(c == jnp.where(r < n, 0, 1), 1.0, 0.0).astype(BF16)
        return jnp.concatenate([jnp.concatenate([vv * sel0, vv * sel1], axis=0), ones], axis=1)

    def normalise(acc):
        head0 = lax.broadcasted_iota(jnp.int32, (acc.shape[0], pair_v), 1) < V_HEAD
        inv = jnp.where(head0, 1.0 / acc[:, pair_v:pair_v + 1], 1.0 / acc[:, pair_v + 1:pair_v + 2])
        return (acc[:, :pair_v] * inv).astype(BF16)

    k_meta = [k_ref[0, 0:N_META, qk_sl[hh]] for hh in heads]
    v_meta = v_ext(v_ref[0, 0:N_META, :])

    ps = []
    for hh in heads:
        s = _dot_nt(q_ref[0, 0:N_META, qk_sl[hh]], k_meta[hh])
        ps.append(jnp.exp2(s - jnp.max(s, axis=1, keepdims=True)))
    o_ref[0, 0:N_META, :] = normalise(_dot(jnp.concatenate(ps, axis=1).astype(BF16), v_meta))

    kc = lax.broadcasted_iota(jnp.int32, (Q_BLOCK, Q_BLOCK), 0) // CHUNK
    qc = lax.broadcasted_iota(jnp.int32, (Q_BLOCK, Q_BLOCK), 1) // CHUNK
    diag_mask = kc <= qc
    eye = jnp.where(lax.broadcasted_iota(jnp.int32, (pair_v, pair_v), 0)
                    == lax.broadcasted_iota(jnp.int32, (pair_v, pair_v), 1), 1.0, 0.0).astype(BF16)

    def vt_ext(vv):
        vt = _dot_nt(eye, vv).astype(BF16)
        ones = jnp.ones((SEG_ALIGN, vv.shape[0]), BF16)
        return [jnp.concatenate([vt[hh * V_HEAD:(hh + 1) * V_HEAD], ones], axis=0) for hh in heads]

    vt_meta = vt_ext(v_ref[0, 0:N_META, :])
    n_kblk = n_qblk
    for j in range(n_kblk):
        c0 = N_META + j * Q_BLOCK
        blk = vt_ext(v_ref[0, c0:c0 + Q_BLOCK, :])
        for hh in heads:
            vt_ref[j, hh] = blk[hh]

    def fold(s):
        return jnp.max(s.reshape(s.shape[0] // SUBLANES, SUBLANES, s.shape[1]), axis=0)

    for i in range(n_qblk):
        r0 = N_META + i * Q_BLOCK
        qs = [q_ref[0, r0:r0 + Q_BLOCK, qk_sl[hh]] for hh in heads]
        s_meta = [_dot_nt(k_meta[hh], qs[hh]) for hh in heads]
        mf = [None, None]
        for j in range(i + 1):
            c0 = N_META + j * Q_BLOCK
            for hh in heads:
                s = _dot_nt(k_ref[0, c0:c0 + Q_BLOCK, qk_sl[hh]], qs[hh])
                if j == i:
                    s = jnp.where(diag_mask, s, MASK_VALUE)
                s_ref[j, hh] = s
                mf[hh] = fold(s) if mf[hh] is None else jnp.maximum(mf[hh], fold(s))
        outs = []
        for hh in heads:
            m = jnp.maximum(jnp.max(mf[hh], axis=0, keepdims=True), jnp.max(s_meta[hh], axis=0, keepdims=True))
            acc = _dot(vt_meta[hh], jnp.exp2(s_meta[hh] - m).astype(BF16))
            for j in range(i + 1):
                acc = acc + _dot(vt_ref[j, hh], jnp.exp2(s_ref[j, hh] - m).astype(BF16))
            outs.append(acc[:V_HEAD] * (1.0 / acc[V_HEAD:V_HEAD + 1]))
        o_ref[0, r0:r0 + Q_BLOCK, :] = jnp.concatenate(outs, axis=0).T.astype(BF16)


def _attention(q, k, v):
    bsz, t, _ = q.shape
    n_blk = (t - N_META) // Q_BLOCK
    qk_spec = pl.BlockSpec((1, t, 2 * HEAD_PAD), lambda b, p: (b, 0, p))
    v_spec = pl.BlockSpec((1, t, 2 * V_HEAD), lambda b, p: (b, 0, p))
    return pl.pallas_call(
        _attn_kernel,
        grid=(bsz, N_HEADS // 2),
        in_specs=[qk_spec, qk_spec, v_spec],
        out_specs=v_spec,
        out_shape=jax.ShapeDtypeStruct((bsz, t, N_HEADS * V_HEAD), BF16),
        scratch_shapes=[pltpu.VMEM((n_blk, 2, Q_BLOCK, Q_BLOCK), F32),
                        pltpu.VMEM((n_blk, 2, V_HEAD + SEG_ALIGN, Q_BLOCK), BF16)],
        compiler_params=_cparams(2),
        name="attention",
    )(q, k, v)


def _final_kernel(h_ref, g_ref, o_ref):
    o_ref[0] = _rms(h_ref[0, N_META:, :], g_ref[...])


def _final_norm(h, g):
    bsz, t, d = h.shape
    return pl.pallas_call(
        _final_kernel,
        grid=(bsz,),
        in_specs=[pl.BlockSpec((1, t, d), lambda b: (b, 0, 0)), pl.BlockSpec((1, d), lambda b: (0, 0))],
        out_specs=pl.BlockSpec((1, t - N_META, d), lambda b: (b, 0, 0)),
        out_shape=jax.ShapeDtypeStruct((bsz, t - N_META, d), F32),
        compiler_params=_cparams(1),
        name="final_norm",
    )(h, g)


def _rot_cols(w):
    half = w.shape[1] // 2
    return jnp.concatenate([-w[:, half:], w[:, :half]], axis=1)


def _gate_pairs(w_a, w_i):
    def bd(w):
        z = jnp.zeros((RNN_BLOCK, RNN_BLOCK), w.dtype)
        return jnp.stack([jnp.block([[w[2 * p], z], [z, w[2 * p + 1]]]) for p in range(N_RNN_BLOCKS // 2)])
    return jnp.concatenate([bd(w_a), bd(w_i)], axis=2).astype(BF16)


def _q_up_aug(w_uq):
    w = w_uq.reshape(Q_LORA, N_HEADS, QK_NOPE + QK_ROPE)
    pe = w[:, :, QK_NOPE:]
    rot = jnp.concatenate([-pe[:, :, QK_ROPE // 2:], pe[:, :, :QK_ROPE // 2]], axis=2)
    return jnp.concatenate([w, rot], axis=2).reshape(Q_LORA, N_HEADS * HEAD_PAD).astype(BF16)


def _kv_up_aug(w_up):
    w = w_up.reshape(KV_LORA, N_HEADS, QK_NOPE + V_HEAD)
    pad = HEAD_PAD - QK_NOPE
    w_k_top = jnp.concatenate([w[:, :, :QK_NOPE], jnp.zeros((KV_LORA, N_HEADS, pad), w.dtype)], axis=2)
    eye = jnp.concatenate([jnp.zeros((QK_ROPE, QK_NOPE), w.dtype), jnp.eye(QK_ROPE, dtype=w.dtype),
                           jnp.zeros((QK_ROPE, pad - QK_ROPE), w.dtype)], axis=1)
    w_k_bot = jnp.broadcast_to(eye[:, None, :], (QK_ROPE, N_HEADS, HEAD_PAD))
    w_k = jnp.concatenate([w_k_top, w_k_bot], axis=0).reshape(KV_LORA + QK_ROPE, N_HEADS * HEAD_PAD)
    w_v = w[:, :, QK_NOPE:].reshape(KV_LORA, N_HEADS * V_HEAD)
    return w_k.astype(BF16), w_v.astype(BF16)


def _rope_tables(t):
    inv_freq = ROPE_THETA ** (-jnp.arange(0, QK_ROPE, 2, dtype=F32) / QK_ROPE)
    ang = jnp.arange(t, dtype=F32)[:, None] * inv_freq[None, :]
    cos = jnp.tile(jnp.cos(ang), (1, 2))
    sin = jnp.tile(jnp.sin(ang), (1, 2))
    cs_k = jnp.concatenate([cos, sin], axis=1)
    ones = jnp.ones((t, QK_NOPE), F32)
    zeros = jnp.zeros((t, HEAD_PAD - QK_NOPE - QK_ROPE), F32)
    ctab = SCORE_SCALE * jnp.concatenate([ones, cos, zeros], axis=1)
    stab = SCORE_SCALE * jnp.concatenate([0.0 * ones, sin, zeros], axis=1)
    return cs_k, ctab, stab


def kernel(x, meta_tokens, norm_mix, norm_ffn, norm_final, rnn_w_in, rnn_conv_w, rnn_conv_b, rnn_w_a, rnn_b_a, rnn_w_i, rnn_b_i, rnn_lambda, rnn_w_out, kv_norm_src, kv_w_down, kv_latent_norm, kv_w_up, q_w_down, q_latent_norm, q_w_up, attn_w_out, ffn_w1, ffn_w3, ffn_w2, moe_router, moe_w1, moe_w3, moe_w2):
    h = x
    t = x.shape[1] + N_META
    assert t % TT == 0 and (t - N_META) % Q_BLOCK == 0
    row = lambda v: v.reshape(1, -1)
    cs_k, ctab, stab = _rope_tables(t)

    ffn_w = [w.astype(BF16) for w in (ffn_w1, ffn_w3, ffn_w2)]
    moe_w = [w.astype(BF16) for w in (moe_w1, moe_w3, moe_w2)]
    k = v = attn = w_o = None
    for layer in range(DEPTH):
        if layer < N_A_LAYERS:
            a = layer
            meta = meta_tokens.astype(x.dtype) if layer == 0 else None
            h = _rnn_layer(h, meta, row(norm_mix[layer]), rnn_w_in[a].astype(BF16), rnn_conv_w[a], row(rnn_conv_b[a]),
                           _gate_pairs(rnn_w_a[a], rnn_w_i[a]), row(rnn_b_a[a]), row(rnn_b_i[a]),
                           row(rnn_lambda[a]), rnn_w_out[a].astype(BF16))
        else:
            b = layer - N_A_LAYERS
            if b == 0:
                w_down_aug = jnp.concatenate([kv_w_down, _rot_cols(kv_w_down[:, KV_LORA:])], axis=1).astype(BF16)
                w_k, w_v = _kv_up_aug(kv_w_up)
                k, v = _kv_proj(h, row(kv_norm_src), w_down_aug, row(kv_latent_norm), cs_k, w_k, w_v)
            q = _q_proj(h, row(norm_mix[layer]), q_w_down[b].astype(BF16), row(q_latent_norm[b]),
                        _q_up_aug(q_w_up[b]), ctab, stab)
            attn, w_o = _attention(q, k, v), attn_w_out[b].astype(BF16)
        if layer % 2 == 0:
            h = _ffn_layer(h, attn, w_o, row(norm_ffn[layer]), *ffn_w, e=layer // 2)
        else:
            h = _moe_layer(h, attn, w_o, row(norm_ffn[layer]), moe_router[layer // 2], *moe_w,
                           layer=layer // 2, tf=1792)
        attn = None
    return _final_norm(h, row(norm_final))
```

```python
import functools
import math

import jax
import jax.numpy as jnp
from jax import lax
from jax.experimental import pallas as pl
from jax.experimental.pallas import tpu as pltpu

D_MODEL = 1024
N_META = 16
CHUNK = 64
NORM_EPS = 1e-6
DEPTH = 4
N_A_LAYERS = DEPTH // 2

D_RNN = D_MODEL
N_RNN_BLOCKS = 8
RNN_BLOCK = D_RNN // N_RNN_BLOCKS
CONV_WIDTH = 4
LRU_C = 8.0

N_HEADS = 16
QK_NOPE = 64
QK_ROPE = 32
V_HEAD = 64
Q_LORA = 384
KV_LORA = 256
ROPE_THETA = 10000.0
ATTN_SCALE = 1.0 / math.sqrt(QK_NOPE + QK_ROPE)
MASK_VALUE = -1e30
SCORE_SCALE = ATTN_SCALE * math.log2(math.e)

N_EXPERTS = 8

LANES = 128
SUBLANES = 8
VMEM_LIMIT = 56 * 1024 * 1024

HEAD_PAD = 128
Q_BLOCK = 512
TT = 688
FF_CHUNK = 1024
SCAN_GROUPS = 43
TM = 512
SEG_ALIGN = 16
SEG_SIZES = (512, 256, 128, 64, 32, 16)
TILE_ROWS = 2 * TM + N_EXPERTS * SEG_ALIGN
F32 = jnp.float32
BF16 = jnp.bfloat16


def _cparams(n_axes):
    return pltpu.CompilerParams(dimension_semantics=("arbitrary",) * n_axes,
                                vmem_limit_bytes=VMEM_LIMIT)


def _rms(x, g):
    return x * lax.rsqrt(jnp.mean(x * x, axis=-1, keepdims=True) + NORM_EPS) * g


def _sigmoid(x):
    return 0.5 * jnp.tanh(0.5 * x) + 0.5


def _dot(a, b):
    return jnp.dot(a, b, preferred_element_type=F32)


def _dot_nt(a, b):
    return lax.dot_general(a, b, (((1,), (1,)), ((), ())), preferred_element_type=F32)


def _rnn_kernel(*refs, prepend_meta):
    if prepend_meta:
        (h_ref, meta_ref, g_ref, win_ref, cw_ref, cb_ref, wg_ref, ba_ref, bi_ref, lam_ref, wout_ref,
         o_ref, xpad_ref, yg_ref, a_ref, u_ref, carry_ref, head_ref) = refs
    else:
        (h_ref, g_ref, win_ref, cw_ref, cb_ref, wg_ref, ba_ref, bi_ref, lam_ref, wout_ref,
         o_ref, xpad_ref, yg_ref, a_ref, u_ref, carry_ref) = refs
    t = pl.program_id(1)
    tt = h_ref.shape[1]

    @pl.when(t == 0)
    def _():
        xpad_ref[0:SUBLANES, :] = jnp.zeros((SUBLANES, D_RNN), F32)
        carry_ref[...] = jnp.zeros_like(carry_ref)
        if prepend_meta:
            head_ref[...] = meta_ref[...]

    if prepend_meta:
        x = jnp.concatenate([head_ref[...], h_ref[0, :tt - N_META, :]], axis=0)

        @pl.when(t < pl.num_programs(1) - 1)
        def _():
            head_ref[...] = h_ref[0, tt - N_META:, :]
    else:
        x = h_ref[0]
    hn = _rms(x, g_ref[...]).astype(BF16)
    yg_ref[...] = jax.nn.gelu(_dot(hn, win_ref[:, :D_RNN]), approximate=True)
    xpad_ref[SUBLANES:, :] = _dot(hn, win_ref[:, D_RNN:])

    xc = cb_ref[...] + cw_ref[CONV_WIDTH - 1:CONV_WIDTH, :] * xpad_ref[SUBLANES:, :]
    for j in range(CONV_WIDTH - 1):
        off = SUBLANES - (CONV_WIDTH - 1) + j
        xc = xc + cw_ref[j:j + 1, :] * xpad_ref[off:off + tt, :]
    xpad_ref[0:SUBLANES, :] = xpad_ref[tt:tt + SUBLANES, :]

    xcb = xc.astype(BF16)
    log_coef = -LRU_C * jax.nn.softplus(-lam_ref[...])
    pair = 2 * RNN_BLOCK
    for p in range(N_RNN_BLOCKS // 2):
        sl = slice(p * pair, (p + 1) * pair)
        gates = _dot(xcb[:, sl], wg_ref[p])
        gate_r = _sigmoid(gates[:, :pair] + ba_ref[:, sl])
        gate_i = _sigmoid(gates[:, pair:] + bi_ref[:, sl])
        log_a = log_coef[:, sl] * gate_r
        a_ref[:, sl] = jnp.exp(log_a)
        th = jnp.tanh(log_a)
        u_ref[:, sl] = jnp.sqrt(-2.0 * th / (1.0 - th)) * (gate_i * xc[:, sl])

    row = lax.broadcasted_iota(jnp.int32, (SUBLANES, D_RNN), 0)

    def groups(i, carry):
        local = []
        for g in range(SCAN_GROUPS):
            r0 = pl.multiple_of((i * SCAN_GROUPS + g) * SUBLANES, SUBLANES)
            a = a_ref[pl.ds(r0, SUBLANES), :]
            u = u_ref[pl.ds(r0, SUBLANES), :]
            for s in (1, 2, 4):
                keep = row >= s
                u = jnp.where(keep, a * pltpu.roll(u, s, axis=0) + u, u)
                a = jnp.where(keep, a * pltpu.roll(a, s, axis=0), a)
            local.append((r0, a, u))
        for r0, a, u in local:
            hs = a * carry + u
            u_ref[pl.ds(r0, SUBLANES), :] = hs
            carry = hs[SUBLANES - 1:SUBLANES, :]
        return carry

    carry_ref[...] = lax.fori_loop(0, tt // (SCAN_GROUPS * SUBLANES), groups, carry_ref[...])

    y = (u_ref[...] * yg_ref[...]).astype(BF16)
    o_ref[0] = x + _dot(y, wout_ref[...])


def _rnn_layer(h, meta, g, w_in, conv_w, conv_b, w_gate, b_a, b_i, lam, w_out):
    bsz, t, d = h.shape
    if meta is not None:
        t += N_META
    nt = t // TT
    full = lambda shape: pl.BlockSpec(shape, lambda b, i: (0,) * len(shape))
    tile = pl.BlockSpec((1, TT, d), lambda b, i: (b, i, 0))
    meta_args, meta_specs, meta_scratch = (), [], []
    if meta is not None:
        meta_args, meta_specs = (meta,), [full((N_META, d))]
        meta_scratch = [pltpu.VMEM((N_META, d), F32)]
    return pl.pallas_call(
        functools.partial(_rnn_kernel, prepend_meta=meta is not None),
        grid=(bsz, nt),
        in_specs=[tile] + meta_specs +
                 [full((1, d)), full((d, 2 * D_RNN)), full((CONV_WIDTH, D_RNN)), full((1, D_RNN)),
                  full(w_gate.shape), full((1, D_RNN)), full((1, D_RNN)), full((1, D_RNN)),
                  full((D_RNN, d))],
        out_specs=tile,
        out_shape=jax.ShapeDtypeStruct((bsz, t, d), F32),
        scratch_shapes=[pltpu.VMEM((TT + SUBLANES, D_RNN), F32), pltpu.VMEM((TT, D_RNN), F32),
                        pltpu.VMEM((TT, D_RNN), F32), pltpu.VMEM((TT, D_RNN), F32),
                        pltpu.VMEM((1, D_RNN), F32)] + meta_scratch,
        compiler_params=_cparams(2),
        name="rnn_mixer",
    )(h, *meta_args, g, w_in, conv_w, conv_b, w_gate, b_a, b_i, lam, w_out)


def _ffn_kernel(*refs, with_attn):
    if with_attn:
        h_ref, a_ref, wo_ref, g_ref, w1_ref, w3_ref, w2_ref, o_ref = refs
    else:
        h_ref, g_ref, w1_ref, w3_ref, w2_ref, o_ref = refs
    out = h_ref[0]
    if with_attn:
        out = out + _dot(a_ref[0], wo_ref[...])
    hf = _rms(out, g_ref[...]).astype(BF16)
    d_ff = w1_ref.shape[1]
    for c0 in range(0, d_ff, FF_CHUNK):
        c1 = min(c0 + FF_CHUNK, d_ff)
        act = jax.nn.silu(_dot(hf, w1_ref[:, c0:c1])) * _dot(hf, w3_ref[:, c0:c1])
        out = out + _dot(act.astype(BF16), w2_ref[c0:c1, :])
    o_ref[0] = out


def _ffn_layer(h, attn, w_o, g, w1, w3, w2, e):
    bsz, t, d = h.shape
    d_ff = w1.shape[2]
    nt = t // TT
    tile = pl.BlockSpec((1, TT, d), lambda b, i: (b, i, 0))
    resident = lambda shape, idx: pl.BlockSpec(shape, lambda b, i: idx, pipeline_mode=pl.Buffered(1))
    attn_args, attn_specs = (), []
    if attn is not None:
        attn_args = (attn, w_o)
        attn_specs = [pl.BlockSpec((1, TT, attn.shape[2]), lambda b, i: (b, i, 0)),
                      resident(w_o.shape, (0, 0))]
    return pl.pallas_call(
        functools.partial(_ffn_kernel, with_attn=attn is not None),
        grid=(bsz, nt),
        in_specs=[tile] + attn_specs +
                 [pl.BlockSpec((1, d), lambda b, i: (0, 0)),
                  resident((None, d, d_ff), (e, 0, 0)),
                  resident((None, d, d_ff), (e, 0, 0)),
                  resident((None, d_ff, d), (e, 0, 0))],
        out_specs=tile,
        out_shape=jax.ShapeDtypeStruct(h.shape, F32),
        compiler_params=_cparams(2),
        name="ffn_dense",
    )(h, *attn_args, g, w1, w3, w2)


def _router_kernel(*refs, with_attn):
    if with_attn:
        h_ref, a_ref, wo_ref, g_ref, wrt_ref, hmid_ref, hf_ref, route_ref, cnt_ref = refs
        hmid = h_ref[...] + _dot(a_ref[...], wo_ref[...])
        hmid_ref[...] = hmid
    else:
        h_ref, g_ref, wrt_ref, hf_ref, route_ref, cnt_ref = refs
        hmid = h_ref[...]
    hf = _rms(hmid, g_ref[...])
    hi = hf.astype(BF16)
    hf_ref[...] = hi
    lo = (hf - hi.astype(F32)).astype(BF16)
    w = wrt_ref[...]
    w_hi = w.astype(BF16)
    w_split = jnp.concatenate([w_hi, (w - w_hi.astype(F32)).astype(BF16)], axis=0)
    part_hi = _dot_nt(w_split, hi)
    logits = part_hi[:N_EXPERTS] + part_hi[N_EXPERTS:] + _dot_nt(w_split, lo)[:N_EXPERTS]
    eidx = lax.broadcasted_iota(jnp.int32, logits.shape, 0)
    m1 = jnp.max(logits, axis=0, keepdims=True)
    i1 = jnp.min(jnp.where(logits == m1, eidx, N_EXPERTS), axis=0, keepdims=True)
    rest = jnp.where(eidx == i1, -jnp.inf, logits)
    m2 = jnp.max(rest, axis=0, keepdims=True)
    i2 = jnp.min(jnp.where(rest == m2, eidx, N_EXPERTS), axis=0, keepdims=True)
    ex = jnp.exp(m2 - m1)
    inv = 1.0 / (1.0 + ex)
    sel = jnp.where(eidx == i1, 1.0, 0.0) + jnp.where(eidx == i2, 1.0, 0.0)
    cnt_ref[0] = jnp.broadcast_to(jnp.sum(sel, axis=1, keepdims=True), cnt_ref.shape[1:])
    route_ref[0] = jnp.where(eidx == 0, i1.astype(F32), jnp.where(eidx == 1, i2.astype(F32),
                             jnp.where(eidx == 2, inv, jnp.where(eidx == 3, ex * inv, 0.0))))


def _router(h2d, attn2d, w_o, g, w_router_t):
    m, d = h2d.shape
    nt = m // TM
    rows = pl.BlockSpec((TM, d), lambda t: (t, 0))
    attn_args, attn_specs, hmid_specs, hmid_shapes = (), [], [], []
    if attn2d is not None:
        attn_args = (attn2d, w_o)
        attn_specs = [pl.BlockSpec((TM, attn2d.shape[1]), lambda t: (t, 0)), pl.BlockSpec(w_o.shape, lambda t: (0, 0))]
        hmid_specs, hmid_shapes = [rows], [jax.ShapeDtypeStruct((m, d), F32)]
    outs = pl.pallas_call(
        functools.partial(_router_kernel, with_attn=attn2d is not None),
        grid=(nt,),
        in_specs=[rows] + attn_specs + [pl.BlockSpec((1, d), lambda t: (0, 0)),
                                        pl.BlockSpec((N_EXPERTS, d), lambda t: (0, 0))],
        out_specs=hmid_specs + [rows,
                                pl.BlockSpec((1, N_EXPERTS, TM), lambda t: (t, 0, 0)),
                                pl.BlockSpec((1, N_EXPERTS, LANES), lambda t: (t, 0, 0))],
        out_shape=hmid_shapes + [jax.ShapeDtypeStruct((m, d), BF16),
                                 jax.ShapeDtypeStruct((nt, N_EXPERTS, TM), F32),
                                 jax.ShapeDtypeStruct((nt, N_EXPERTS, LANES), F32)],
        compiler_params=_cparams(1),
        name="moe_router",
    )(h2d, *attn_args, g, w_router_t)
    return tuple(outs) if attn2d is not None else (h2d, *outs)


def _segment_copies(t, seg_ref, len_ref, base_ref, make_copy, start):
    for e in range(N_EXPERTS):
        n = len_ref[t * N_EXPERTS + e]
        local0 = seg_ref[t * N_EXPERTS + e]
        global0 = base_ref[t * N_EXPERTS + e]
        off = 0
        for k, p in enumerate(SEG_SIZES):
            take = (n & p) != 0

            @pl.when(take)
            def _(off=off, p=p, k=k):
                cp = make_copy(pl.multiple_of(local0 + off, SEG_ALIGN), pl.multiple_of(global0 + off, SEG_ALIGN), p)
                if start:
                    cp.start(priority=(e + k) % 2)
                else:
                    cp.wait()

            off = off + jnp.where(take, p, 0)


def _zero_unused_rows(gap_ref, na_ref, xs_ref, zero_ref, zsem):
    zero_ref[...] = jnp.zeros_like(zero_ref)
    n_tiles = xs_ref.shape[0] // TM

    def gap_copies(start):
        for e in range(N_EXPERTS):
            n = gap_ref[N_EXPERTS + e]
            off = 0
            for p in SEG_SIZES[1:]:
                take = (n & p) != 0

                @pl.when(take)
                def _(off=off, p=p, e=e):
                    dst = pl.multiple_of(gap_ref[e] + off, SEG_ALIGN)
                    cp = pltpu.make_async_copy(zero_ref.at[pl.ds(0, p)], xs_ref.at[pl.ds(dst, p)], zsem)
                    cp.start() if start else cp.wait()

                off = off + jnp.where(take, p, 0)

    def tile_copy(i, carry, start):
        dst = pl.multiple_of(i * TM, TM)
        cp = pltpu.make_async_copy(zero_ref, xs_ref.at[pl.ds(dst, TM)], zsem)
        cp.start() if start else cp.wait()
        return carry

    gap_copies(True)
    lax.fori_loop(na_ref[0], n_tiles, functools.partial(tile_copy, start=True), 0)
    gap_copies(False)
    lax.fori_loop(na_ref[0], n_tiles, functools.partial(tile_copy, start=False), 0)


def _dispatch_kernel(seg_ref, len_ref, base_ref, gap_ref, na_ref, hf_ref, route_ref, tri_ref, xs_ref, slot_ref,
                     stage_ref, zero_ref, sem, zsem):
    t = pl.program_id(0)
    route = route_ref[0]
    eidx = lax.broadcasted_iota(jnp.int32, route.shape, 0)
    sel1 = eidx == route[0:1, :].astype(jnp.int32)
    sel2 = eidx == route[1:2, :].astype(jnp.int32)
    sel = jnp.where(sel1, 1.0, 0.0) + jnp.where(sel2, 1.0, 0.0)
    sel16 = jnp.concatenate([sel, jnp.zeros_like(sel)], axis=0).astype(BF16)
    rank = _dot(sel16, tri_ref[...])[:N_EXPERTS]
    seg = jnp.zeros(route.shape, jnp.int32)
    for e in range(N_EXPERTS):
        seg = jnp.where(eidx == e, seg_ref[t * N_EXPERTS + e], seg)
    slot = seg.astype(F32) + rank
    slot1 = jnp.sum(jnp.where(sel1, slot, 0.0), axis=0, keepdims=True)
    slot2 = jnp.sum(jnp.where(sel2, slot, 0.0), axis=0, keepdims=True)
    slot_ref[0] = jnp.where(eidx == 0, slot1, jnp.where(eidx == 1, slot2, 0.0))
    r = lax.broadcasted_iota(jnp.int32, (TILE_ROWS, TM), 0)
    perm = jnp.where(r == slot1.astype(jnp.int32), 1.0, jnp.where(r == slot2.astype(jnp.int32), 1.0, 0.0))
    buf = t % 2
    stage_ref[buf] = _dot(perm.astype(BF16), hf_ref[...]).astype(BF16)

    def store(tile, b, start):
        def copy(local, glob, p):
            return pltpu.make_async_copy(stage_ref.at[b, pl.ds(local, p)], xs_ref.at[pl.ds(glob, p)], sem.at[b])

        _segment_copies(tile, seg_ref, len_ref, base_ref, copy, start=start)

    store(t, buf, start=True)

    @pl.when(t > 0)
    def _():
        store(t - 1, 1 - buf, start=False)

    @pl.when(t == pl.num_programs(0) - 1)
    def _():
        store(t, buf, start=False)
        _zero_unused_rows(gap_ref, na_ref, xs_ref, zero_ref, zsem)


def _dispatch(seg, seg_len, base, gaps, n_active, hf, route, tri, n_tiles_max):
    m, d = hf.shape
    nt = m // TM
    grid_spec = pltpu.PrefetchScalarGridSpec(
        num_scalar_prefetch=5,
        grid=(nt,),
        in_specs=[pl.BlockSpec((TM, d), lambda t, *_: (t, 0)),
                  pl.BlockSpec((1, N_EXPERTS, TM), lambda t, *_: (t, 0, 0)),
                  pl.BlockSpec((TM, TM), lambda t, *_: (0, 0))],
        out_specs=[pl.BlockSpec(memory_space=pl.ANY),
                   pl.BlockSpec((1, N_EXPERTS, TM), lambda t, *_: (t, 0, 0))],
        scratch_shapes=[pltpu.VMEM((2, TILE_ROWS, d), BF16), pltpu.VMEM((TM, d), BF16),
                        pltpu.SemaphoreType.DMA((2,)), pltpu.SemaphoreType.DMA(())],
    )
    return pl.pallas_call(
        _dispatch_kernel,
        grid_spec=grid_spec,
        out_shape=[jax.ShapeDtypeStruct((n_tiles_max * TM, d), BF16),
                   jax.ShapeDtypeStruct((nt, N_EXPERTS, TM), F32)],
        compiler_params=_cparams(1),
        name="moe_dispatch",
    )(seg, seg_len, base, gaps, n_active, hf, route, tri)


def _experts_kernel(te_ref, na_ref, x_ref, w1_ref, w3_ref, w2_ref, o_ref, acc_ref, *, nf):
    del te_ref
    i = pl.program_id(0)
    f = pl.program_id(1)

    active = i < na_ref[0]
    last = nf - 1

    def partial_out():
        x = x_ref[...]
        act = jax.nn.silu(_dot(x, w1_ref[0])) * _dot(x, w3_ref[0])
        return _dot(act.astype(BF16), w2_ref[0])

    @pl.when(jnp.logical_and(active, f == 0))
    def _():
        acc_ref[...] = partial_out()

    if nf > 2:
        @pl.when(jnp.logical_and(active, jnp.logical_and(f > 0, f < last)))
        def _():
            acc_ref[...] += partial_out()

    @pl.when(jnp.logical_and(active, f == last))
    def _():
        o_ref[...] = (acc_ref[...] + partial_out()).astype(BF16)

    @pl.when(jnp.logical_and(i >= na_ref[0], f == last))
    def _():
        o_ref[...] = jnp.zeros_like(o_ref)


def _experts(tile_expert, n_active, x_sorted, w1, w3, w2, layer, tf):
    rows, d = x_sorted.shape
    d_ff = w1.shape[3]
    n_tiles, nf = rows // TM, d_ff // tf
    row_blk = lambda i, f, te, na: (jnp.minimum(i, na[0] - 1), 0)
    chunk = lambda i, f, na: jnp.where(i < na[0], f, nf - 1)
    grid_spec = pltpu.PrefetchScalarGridSpec(
        num_scalar_prefetch=2,
        grid=(n_tiles, nf),
        in_specs=[pl.BlockSpec((TM, d), row_blk),
                  pl.BlockSpec((None, 1, d, tf), lambda i, f, te, na: (layer, te[i], 0, chunk(i, f, na))),
                  pl.BlockSpec((None, 1, d, tf), lambda i, f, te, na: (layer, te[i], 0, chunk(i, f, na))),
                  pl.BlockSpec((None, 1, tf, d), lambda i, f, te, na: (layer, te[i], chunk(i, f, na), 0))],
        out_specs=pl.BlockSpec((TM, d), lambda i, f, te, na: (i, 0)),
        scratch_shapes=[pltpu.VMEM((TM, d), F32)],
    )
    assert nf >= 2
    return pl.pallas_call(
        functools.partial(_experts_kernel, nf=nf),
        grid_spec=grid_spec,
        out_shape=jax.ShapeDtypeStruct((rows, d), BF16),
        compiler_params=_cparams(2),
        name="moe_experts",
    )(tile_expert, n_active, x_sorted, w1, w3, w2)


def _combine_kernel(seg_ref, len_ref, base_ref, h_ref, slot_ref, gate_ref, y_ref, o_ref, ybuf_ref, sem):
    t = pl.program_id(0)
    buf = t % 2

    def fetch(tile, b, start):
        def copy(local, glob, p):
            return pltpu.make_async_copy(y_ref.at[pl.ds(glob, p)], ybuf_ref.at[b, pl.ds(local, p)], sem.at[b])

        if start:
            ybuf_ref[b, 2 * TM:, :] = jnp.zeros((TILE_ROWS - 2 * TM, ybuf_ref.shape[2]), BF16)
        _segment_copies(tile, seg_ref, len_ref, base_ref, copy, start=start)

    @pl.when(t == 0)
    def _():
        fetch(t, buf, start=True)

    @pl.when(t + 1 < pl.num_programs(0))
    def _():
        fetch(t + 1, 1 - buf, start=True)

    fetch(t, buf, start=False)

    y = ybuf_ref[buf]
    lane = lax.broadcasted_iota(jnp.int32, (TM, TILE_ROWS), 1)
    out = h_ref[...]
    for k in range(2):
        onehot = jnp.where(lane == slot_ref[:, k:k + 1], 1.0, 0.0).astype(BF16)
        out = out + gate_ref[:, k:k + 1] * _dot(onehot, y)
    o_ref[...] = out


def _combine(seg, seg_len, base, h2d, slots, gates, y_sorted):
    m, d = h2d.shape
    nt = m // TM
    grid_spec = pltpu.PrefetchScalarGridSpec(
        num_scalar_prefetch=3,
        grid=(nt,),
        in_specs=[pl.BlockSpec((TM, d), lambda t, *_: (t, 0)),
                  pl.BlockSpec((TM, 2), lambda t, *_: (t, 0)),
                  pl.BlockSpec((TM, 2), lambda t, *_: (t, 0)),
                  pl.BlockSpec(memory_space=pl.ANY)],
        out_specs=pl.BlockSpec((TM, d), lambda t, *_: (t, 0)),
        scratch_shapes=[pltpu.VMEM((2, TILE_ROWS, d), BF16), pltpu.SemaphoreType.DMA((2,))],
    )
    return pl.pallas_call(
        _combine_kernel,
        grid_spec=grid_spec,
        out_shape=jax.ShapeDtypeStruct((m, d), F32),
        compiler_params=_cparams(1),
        name="moe_combine",
    )(seg, seg_len, base, h2d, slots, gates, y_sorted)


def _moe_tables(counts, n_tiles_max):
    pad = (counts + SEG_ALIGN - 1) // SEG_ALIGN * SEG_ALIGN
    seg = jnp.cumsum(pad, axis=1) - pad
    used = jnp.sum(pad, axis=0)
    region = (used + TM - 1) // TM * TM
    region_start = jnp.cumsum(region) - region
    base = region_start[None, :] + jnp.cumsum(pad, axis=0) - pad
    gaps = jnp.concatenate([region_start + used, region - used])
    tiles_end = jnp.cumsum(region // TM)
    n_active = tiles_end[-1]
    tile = jnp.minimum(jnp.arange(n_tiles_max), n_active - 1)
    tile_expert = jnp.sum(tile[:, None] >= tiles_end[None, :], axis=1)
    flat = lambda a: a.reshape(-1).astype(jnp.int32)
    return flat(seg), flat(pad), flat(base), flat(gaps), flat(tile_expert), flat(n_active)


def _moe_layer(h, attn, w_o, g, w_router, w1, w3, w2, layer, tf):
    bsz, t, d = h.shape
    m = bsz * t
    nt = m // TM
    n_tiles_max = -(-(nt * (TILE_ROWS - SEG_ALIGN) + N_EXPERTS * (TM - 1)) // TM)
    attn2d = None if attn is None else attn.reshape(m, attn.shape[2])
    h2d, hf, route, counts = _router(h.reshape(m, d), attn2d, w_o, g, w_router.T)
    seg, seg_len, base, gaps, tile_expert, n_active = _moe_tables(counts[:, :, 0].astype(jnp.int32), n_tiles_max)
    tri = (jnp.arange(TM)[:, None] < jnp.arange(TM)[None, :]).astype(BF16)
    x_sorted, slots = _dispatch(seg, seg_len, base, gaps, n_active, hf, route, tri, n_tiles_max)
    y_sorted = _experts(tile_expert, n_active, x_sorted, w1, w3, w2, layer, tf)
    to_cols = lambda a: a.transpose(0, 2, 1).reshape(m, 2)
    out = _combine(seg, seg_len, base, h2d, to_cols(slots[:, :2, :]).astype(jnp.int32),
                   to_cols(route[:, 2:4, :]), y_sorted)
    return out.reshape(bsz, t, d)


def _kv_kernel(h_ref, g_ref, wd_ref, gl_ref, cs_ref, wk_ref, wv_ref, k_ref, v_ref):
    hs = _rms(h_ref[0], g_ref[...]).astype(BF16)
    ckv = _dot(hs, wd_ref[...])
    c_lat = _rms(ckv[:, :KV_LORA], gl_ref[...])
    pe = ckv[:, KV_LORA:KV_LORA + QK_ROPE]
    pe_rot = ckv[:, KV_LORA + QK_ROPE:]
    k_pe = pe * cs_ref[:, :QK_ROPE] + pe_rot * cs_ref[:, QK_ROPE:]
    lat = jnp.concatenate([c_lat, k_pe], axis=1).astype(BF16)
    k_ref[0] = _dot(lat, wk_ref[...]).astype(BF16)
    v_ref[0] = _dot(lat[:, :KV_LORA], wv_ref[...]).astype(BF16)


def _kv_proj(h, g_src, w_down_aug, g_latent, cs_k, w_k, w_v):
    bsz, t, d = h.shape
    nt = t // TT
    full = lambda shape: pl.BlockSpec(shape, lambda b, i: (0,) * len(shape))
    return pl.pallas_call(
        _kv_kernel,
        grid=(bsz, nt),
        in_specs=[pl.BlockSpec((1, TT, d), lambda b, i: (b, i, 0)), full((1, d)), full(w_down_aug.shape),
                  full((1, KV_LORA)), pl.BlockSpec((TT, 2 * QK_ROPE), lambda b, i: (i, 0)),
                  full(w_k.shape), full(w_v.shape)],
        out_specs=[pl.BlockSpec((1, TT, N_HEADS * HEAD_PAD), lambda b, i: (b, i, 0)),
                   pl.BlockSpec((1, TT, N_HEADS * V_HEAD), lambda b, i: (b, i, 0))],
        out_shape=[jax.ShapeDtypeStruct((bsz, t, N_HEADS * HEAD_PAD), BF16),
                   jax.ShapeDtypeStruct((bsz, t, N_HEADS * V_HEAD), BF16)],
        compiler_params=_cparams(2),
        name="kv_proj",
    )(h, g_src, w_down_aug, g_latent, cs_k, w_k, w_v)


def _q_kernel(h_ref, g_ref, wdq_ref, gq_ref, wuq_ref, ct_ref, st_ref, q_ref):
    hn = _rms(h_ref[0], g_ref[...]).astype(BF16)
    c_q = _rms(_dot(hn, wdq_ref[...]), gq_ref[...]).astype(BF16)
    q = _dot(c_q, wuq_ref[...])
    width = q.shape[1]
    ct = jnp.tile(ct_ref[...], (1, N_HEADS))
    st = jnp.tile(st_ref[...], (1, N_HEADS))
    q_ref[0] = (q * ct + pltpu.roll(q, width - QK_ROPE, axis=1) * st).astype(BF16)


def _q_proj(h, g, w_dq, g_q, w_uq_aug, ctab, stab):
    bsz, t, d = h.shape
    nt = t // TT
    full = lambda shape: pl.BlockSpec(shape, lambda b, i: (0,) * len(shape))
    tab = pl.BlockSpec((TT, HEAD_PAD), lambda b, i: (i, 0))
    return pl.pallas_call(
        _q_kernel,
        grid=(bsz, nt),
        in_specs=[pl.BlockSpec((1, TT, d), lambda b, i: (b, i, 0)), full((1, d)), full(w_dq.shape),
                  full((1, Q_LORA)), full(w_uq_aug.shape), tab, tab],
        out_specs=pl.BlockSpec((1, TT, N_HEADS * HEAD_PAD), lambda b, i: (b, i, 0)),
        out_shape=jax.ShapeDtypeStruct((bsz, t, N_HEADS * HEAD_PAD), BF16),
        compiler_params=_cparams(2),
        name="q_proj",
    )(h, g, w_dq, g_q, w_uq_aug, ctab, stab)


def _attn_kernel(q_ref, k_ref, v_ref, o_ref, s_ref, vt_ref):
    n_qblk = (q_ref.shape[1] - N_META) // Q_BLOCK
    heads = (0, 1)
    qk_sl = [slice(hh * HEAD_PAD, (hh + 1) * HEAD_PAD) for hh in heads]
    pair_v = 2 * V_HEAD

    def v_ext(vv):
        n = vv.shape[0]
        head0 = lax.broadcasted_iota(jnp.int32, (n, pair_v), 1) < V_HEAD
        sel0 = jnp.where(head0, 1.0, 0.0).astype(BF16)
        sel1 = jnp.where(head0, 0.0, 1.0).astype(BF16)
        r = lax.broadcasted_iota(jnp.int32, (2 * n, pair_v), 0)
        c = lax.broadcasted_iota(jnp.int32, (2 * n, pair_v), 1)
        ones = jnp.where(c == jnp.where(r < n, 0, 1), 1.0, 0.0).astype(BF16)
        return jnp.concatenate([jnp.concatenate([vv * sel0, vv * sel1], axis=0), ones], axis=1)

    def normalise(acc):
        head0 = lax.broadcasted_iota(jnp.int32, (acc.shape[0], pair_v), 1) < V_HEAD
        inv = jnp.where(head0, 1.0 / acc[:, pair_v:pair_v + 1], 1.0 / acc[:, pair_v + 1:pair_v + 2])
        return (acc[:, :pair_v] * inv).astype(BF16)

    k_meta = [k_ref[0, 0:N_META, qk_sl[hh]] for hh in heads]
    v_meta = v_ext(v_ref[0, 0:N_META, :])

    ps = []
    for hh in heads:
        s = _dot_nt(q_ref[0, 0:N_META, qk_sl[hh]], k_meta[hh])
        ps.append(jnp.exp2(s - jnp.max(s, axis=1, keepdims=True)))
    o_ref[0, 0:N_META, :] = normalise(_dot(jnp.concatenate(ps, axis=1).astype(BF16), v_meta))

    kc = lax.broadcasted_iota(jnp.int32, (Q_BLOCK, Q_BLOCK), 0) // CHUNK
    qc = lax.broadcasted_iota(jnp.int32, (Q_BLOCK, Q_BLOCK), 1) // CHUNK
    diag_mask = kc <= qc
    eye = jnp.where(lax.broadcasted_iota(jnp.int32, (pair_v, pair_v), 0)
                    == lax.broadcasted_iota(jnp.int32, (pair_v, pair_v), 1), 1.0, 0.0).astype(BF16)

    def vt_ext(vv):
        vt = _dot_nt(eye, vv).astype(BF16)
        ones = jnp.ones((SEG_ALIGN, vv.shape[0]), BF16)
        return [jnp.concatenate([vt[hh * V_HEAD:(hh + 1) * V_HEAD], ones], axis=0) for hh in heads]

    vt_meta = vt_ext(v_ref[0, 0:N_META, :])
    n_kblk = n_qblk
    for j in range(n_kblk):
        c0 = N_META + j * Q_BLOCK
        blk = vt_ext(v_ref[0, c0:c0 + Q_BLOCK, :])
        for hh in heads:
            vt_ref[j, hh] = blk[hh]

    def fold(s):
        return jnp.max(s.reshape(s.shape[0] // SUBLANES, SUBLANES, s.shape[1]), axis=0)

    for i in range(n_qblk):
        r0 = N_META + i * Q_BLOCK
        qs = [q_ref[0, r0:r0 + Q_BLOCK, qk_sl[hh]] for hh in heads]
        s_meta = [_dot_nt(k_meta[hh], qs[hh]) for hh in heads]
        mf = [None, None]
        for j in range(i + 1):
            c0 = N_META + j * Q_BLOCK
            for hh in heads:
                s = _dot_nt(k_ref[0, c0:c0 + Q_BLOCK, qk_sl[hh]], qs[hh])
                if j == i:
                    s = jnp.where(diag_mask, s, MASK_VALUE)
                s_ref[j, hh] = s
                mf[hh] = fold(s) if mf[hh] is None else jnp.maximum(mf[hh], fold(s))
        outs = []
        for hh in heads:
            m = jnp.maximum(jnp.max(mf[hh], axis=0, keepdims=True), jnp.max(s_meta[hh], axis=0, keepdims=True))
            acc = _dot(vt_meta[hh], jnp.exp2(s_meta[hh] - m).astype(BF16))
            for j in range(i + 1):
                acc = acc + _dot(vt_ref[j, hh], jnp.exp2(s_ref[j, hh] - m).astype(BF16))
            outs.append(acc[:V_HEAD] * (1.0 / acc[V_HEAD:V_HEAD + 1]))
        o_ref[0, r0:r0 + Q_BLOCK, :] = jnp.concatenate(outs, axis=0).T.astype(BF16)


def _attention(q, k, v):
    bsz, t, _ = q.shape
    n_blk = (t - N_META) // Q_BLOCK
    qk_spec = pl.BlockSpec((1, t, 2 * HEAD_PAD), lambda b, p: (b, 0, p))
    v_spec = pl.BlockSpec((1, t, 2 * V_HEAD), lambda b, p: (b, 0, p))
    return pl.pallas_call(
        _attn_kernel,
        grid=(bsz, N_HEADS // 2),
        in_specs=[qk_spec, qk_spec, v_spec],
        out_specs=v_spec,
        out_shape=jax.ShapeDtypeStruct((bsz, t, N_HEADS * V_HEAD), BF16),
        scratch_shapes=[pltpu.VMEM((n_blk, 2, Q_BLOCK, Q_BLOCK), F32),
                        pltpu.VMEM((n_blk, 2, V_HEAD + SEG_ALIGN, Q_BLOCK), BF16)],
        compiler_params=_cparams(2),
        name="attention",
    )(q, k, v)


def _final_kernel(h_ref, g_ref, o_ref):
    o_ref[0] = _rms(h_ref[0, N_META:, :], g_ref[...])


def _final_norm(h, g):
    bsz, t, d = h.shape
    return pl.pallas_call(
        _final_kernel,
        grid=(bsz,),
        in_specs=[pl.BlockSpec((1, t, d), lambda b: (b, 0, 0)), pl.BlockSpec((1, d), lambda b: (0, 0))],
        out_specs=pl.BlockSpec((1, t - N_META, d), lambda b: (b, 0, 0)),
        out_shape=jax.ShapeDtypeStruct((bsz, t - N_META, d), F32),
        compiler_params=_cparams(1),
        name="final_norm",
    )(h, g)


def _rot_cols(w):
    half = w.shape[1] // 2
    return jnp.concatenate([-w[:, half:], w[:, :half]], axis=1)


def _gate_pairs(w_a, w_i):
    def bd(w):
        z = jnp.zeros((RNN_BLOCK, RNN_BLOCK), w.dtype)
        return jnp.stack([jnp.block([[w[2 * p], z], [z, w[2 * p + 1]]]) for p in range(N_RNN_BLOCKS // 2)])
    return jnp.concatenate([bd(w_a), bd(w_i)], axis=2).astype(BF16)


def _q_up_aug(w_uq):
    w = w_uq.reshape(Q_LORA, N_HEADS, QK_NOPE + QK_ROPE)
    pe = w[:, :, QK_NOPE:]
    rot = jnp.concatenate([-pe[:, :, QK_ROPE // 2:], pe[:, :, :QK_ROPE // 2]], axis=2)
    return jnp.concatenate([w, rot], axis=2).reshape(Q_LORA, N_HEADS * HEAD_PAD).astype(BF16)


def _kv_up_aug(w_up):
    w = w_up.reshape(KV_LORA, N_HEADS, QK_NOPE + V_HEAD)
    pad = HEAD_PAD - QK_NOPE
    w_k_top = jnp.concatenate([w[:, :, :QK_NOPE], jnp.zeros((KV_LORA, N_HEADS, pad), w.dtype)], axis=2)
    eye = jnp.concatenate([jnp.zeros((QK_ROPE, QK_NOPE), w.dtype), jnp.eye(QK_ROPE, dtype=w.dtype),
                           jnp.zeros((QK_ROPE, pad - QK_ROPE), w.dtype)], axis=1)
    w_k_bot = jnp.broadcast_to(eye[:, None, :], (QK_ROPE, N_HEADS, HEAD_PAD))
    w_k = jnp.concatenate([w_k_top, w_k_bot], axis=0).reshape(KV_LORA + QK_ROPE, N_HEADS * HEAD_PAD)
    w_v = w[:, :, QK_NOPE:].reshape(KV_LORA, N_HEADS * V_HEAD)
    return w_k.astype(BF16), w_v.astype(BF16)


def _rope_tables(t):
    inv_freq = ROPE_THETA ** (-jnp.arange(0, QK_ROPE, 2, dtype=F32) / QK_ROPE)
    ang = jnp.arange(t, dtype=F32)[:, None] * inv_freq[None, :]
    cos = jnp.tile(jnp.cos(ang), (1, 2))
    sin = jnp.tile(jnp.sin(ang), (1, 2))
    cs_k = jnp.concatenate([cos, sin], axis=1)
    ones = jnp.ones((t, QK_NOPE), F32)
    zeros = jnp.zeros((t, HEAD_PAD - QK_NOPE - QK_ROPE), F32)
    ctab = SCORE_SCALE * jnp.concatenate([ones, cos, zeros], axis=1)
    stab = SCORE_SCALE * jnp.concatenate([0.0 * ones, sin, zeros], axis=1)
    return cs_k, ctab, stab


def kernel(x, meta_tokens, norm_mix, norm_ffn, norm_final, rnn_w_in, rnn_conv_w, rnn_conv_b, rnn_w_a, rnn_b_a, rnn_w_i, rnn_b_i, rnn_lambda, rnn_w_out, kv_norm_src, kv_w_down, kv_latent_norm, kv_w_up, q_w_down, q_latent_norm, q_w_up, attn_w_out, ffn_w1, ffn_w3, ffn_w2, moe_router, moe_w1, moe_w3, moe_w2):
    h = x
    t = x.shape[1] + N_META
    assert t % TT == 0 and (t - N_META) % Q_BLOCK == 0
    row = lambda v: v.reshape(1, -1)
    cs_k, ctab, stab = _rope_tables(t)

    ffn_w = [w.astype(BF16) for w in (ffn_w1, ffn_w3, ffn_w2)]
    moe_w = [w.astype(BF16) for w in (moe_w1, moe_w3, moe_w2)]
    k = v = attn = w_o = None
    for layer in range(DEPTH):
        if layer < N_A_LAYERS:
            a = layer
            meta = meta_tokens.astype(x.dtype) if layer == 0 else None
            h = _rnn_layer(h, meta, row(norm_mix[layer]), rnn_w_in[a].astype(BF16), rnn_conv_w[a], row(rnn_conv_b[a]),
                           _gate_pairs(rnn_w_a[a], rnn_w_i[a]), row(rnn_b_a[a]), row(rnn_b_i[a]),
                           row(rnn_lambda[a]), rnn_w_out[a].astype(BF16))
        else:
            b = layer - N_A_LAYERS
            if b == 0:
                w_down_aug = jnp.concatenate([kv_w_down, _rot_cols(kv_w_down[:, KV_LORA:])], axis=1).astype(BF16)
                w_k, w_v = _kv_up_aug(kv_w_up)
                k, v = _kv_proj(h, row(kv_norm_src), w_down_aug, row(kv_latent_norm), cs_k, w_k, w_v)
            q = _q_proj(h, row(norm_mix[layer]), q_w_down[b].astype(BF16), row(q_latent_norm[b]),
                        _q_up_aug(q_w_up[b]), ctab, stab)
            attn, w_o = _attention(q, k, v), attn_w_out[b].astype(BF16)
        if layer % 2 == 0:
            h = _ffn_layer(h, attn, w_o, row(norm_ffn[layer]), *ffn_w, e=layer // 2)
        else:
            h = _moe_layer(h, attn, w_o, row(norm_ffn[layer]), moe_router[layer // 2], *moe_w,
                           layer=layer // 2, tf=1792)
        attn = None
    return _final_norm(h, row(norm_final))
```
